```python
import jax, jax.numpy as jnp
from jax import lax
import numpy as np

D_MODEL = 2048
BATCH = 8
SEQ = 2048
DEPTH = 2

HEAD_DIM = 64
CONV_CH = D_MODEL // 4
CONV_WIDTH = 31
SWA_Q_HEADS = (3 * D_MODEL // 8) // HEAD_DIM
SWA_KV_HEADS = SWA_Q_HEADS // 3
SWA_WINDOW = 128
SWA_WIDTH = SWA_Q_HEADS * HEAD_DIM
DIFF_HEADS = (3 * D_MODEL // 8) // (2 * HEAD_DIM)
DIFF_WIDTH = DIFF_HEADS * 2 * HEAD_DIM
MIX_WIDTH = CONV_CH + SWA_WIDTH + DIFF_WIDTH
IN_WIDTH = 2 * CONV_CH + SWA_WIDTH + 2 * SWA_KV_HEADS * HEAD_DIM + 3 * DIFF_WIDTH
Q_BLOCK = 128
ROPE_THETA = 10000.0
MAX_POS_OFFSET = 4096
N_GROUPS = 8
EXPERTS_PER_GROUP = 8
N_EXPERTS = N_GROUPS * EXPERTS_PER_GROUP
TOP_K_IN_GROUP = 2
EXPERT_FF = D_MODEL // 4
MOE_BLOCK = 128
RMS_EPS = 1e-6
LN_EPS = 1e-5
NEG_INF = -1e30

kernel_name = 'hybrid_conv_swa_diffattn_hmoe_block'


def rms_norm(x, g, eps=RMS_EPS):
    xf = x.astype(jnp.float32)
    y = xf * lax.rsqrt(jnp.mean(xf * xf, axis=-1, keepdims=True) + eps)
    return (y * g.astype(jnp.float32)).astype(x.dtype)


def layer_norm(x, g, b, eps=LN_EPS):
    xf = x.astype(jnp.float32)
    mu = jnp.mean(xf, axis=-1, keepdims=True)
    var = jnp.mean(jnp.square(xf - mu), axis=-1, keepdims=True)
    y = (xf - mu) * lax.rsqrt(var + eps)
    return (y * g.astype(jnp.float32) + b.astype(jnp.float32)).astype(x.dtype)


def rope_tables(positions, dim):
    inv_freq = ROPE_THETA ** (-jnp.arange(0, dim, 2, dtype=jnp.float32) / dim)
    ang = positions.astype(jnp.float32)[..., None] * inv_freq
    return jnp.cos(ang), jnp.sin(ang)


def apply_rope(x, cos, sin):
    xf = x.astype(jnp.float32)
    x1, x2 = jnp.split(xf, 2, axis=-1)
    c = cos[:, :, None, :]
    s = sin[:, :, None, :]
    return jnp.concatenate([x1 * c - x2 * s, x2 * c + x1 * s], axis=-1).astype(x.dtype)


def conv_module(u, dw_w, dw_b, ln_g, ln_b, pw_w):
    a, gate = jnp.split(u, 2, axis=-1)
    z = a * jax.nn.sigmoid(gate)
    z = lax.conv_general_dilated(
        z, dw_w[:, None, :].astype(z.dtype), window_strides=(1,),
        padding=[(CONV_WIDTH - 1, 0)], dimension_numbers=('NWC', 'WIO', 'NWC'),
        feature_group_count=CONV_CH) + dw_b
    z = jax.nn.silu(layer_norm(z, ln_g, ln_b))
    return z @ pw_w


def swa_attention(q, k, v, sinks):
    B, S, Hq, d = q.shape
    Hkv = k.shape[2]
    G = Hq // Hkv
    W = SWA_WINDOW
    nb = S // W
    qb = q.reshape(B, nb, W, Hkv, G, d)

    def with_prev(t):
        t = t.reshape(B, nb, W, Hkv, d)
        prev = jnp.concatenate([jnp.zeros_like(t[:, :1]), t[:, :-1]], axis=1)
        return jnp.concatenate([prev, t], axis=2)

    kk = with_prev(k)
    vv = with_prev(v)
    s = jnp.einsum('bnqhgd,bnkhd->bnhgqk', qb, kk,
                   preferred_element_type=jnp.float32) * (d ** -0.5)
    qi = jnp.arange(W)[:, None] + W
    kj = jnp.arange(2 * W)[None, :]
    rel = qi - kj
    band = (rel >= 0) & (rel < W)
    has_prev = (jnp.arange(nb) > 0)[:, None, None] | (kj >= W)[None]
    valid = (band[None] & has_prev)[None, :, None, None]
    s = jnp.where(valid, s, NEG_INF)
    sink = sinks.astype(jnp.float32).reshape(1, 1, Hkv, G, 1, 1)
    m = jnp.maximum(jnp.max(s, axis=-1, keepdims=True), sink)
    e = jnp.where(valid, jnp.exp(s - m), 0.0)
    p = e / (jnp.sum(e, axis=-1, keepdims=True) + jnp.exp(sink - m))
    o = jnp.einsum('bnhgqk,bnkhd->bnqhgd', p.astype(v.dtype), vv)
    return o.reshape(B, S, Hq * d)


def diff_attention(q, k, v, lam):
    B, S, H, _, d = q.shape
    nb = S // Q_BLOCK
    qb = q.reshape(B, nb, Q_BLOCK, H, 2, d).swapaxes(0, 1)
    kpos = jnp.arange(S)
    scale = d ** -0.5

    def block(args):
        qi, n = args
        s = jnp.einsum('bqhcd,bkhcd->bhcqk', qi, k,
                       preferred_element_type=jnp.float32) * scale
        qpos = n * Q_BLOCK + jnp.arange(Q_BLOCK)
        causal = kpos[None, :] <= qpos[:, None]
        p = jax.nn.softmax(jnp.where(causal, s, NEG_INF), axis=-1)
        a = p[:, :, 0] - lam * p[:, :, 1]
        return jnp.einsum('bhqk,bkhe->bqhe', a.astype(v.dtype), v)

    o = lax.map(block, (qb, jnp.arange(nb)))
    return o.swapaxes(0, 1).reshape(B, S, H, 2 * d)


def hier_moe(h, wg, bg, we, be, w13, w2):
    B, S, D = h.shape
    T = B * S
    xf = h.reshape(T, D)
    x32 = xf.astype(jnp.float32)
    g_prob = jax.nn.softmax(x32 @ wg.astype(jnp.float32) + bg.astype(jnp.float32), axis=-1)
    g_w, g_idx = lax.top_k(g_prob, 1)
    e_logits = (x32 @ we.astype(jnp.float32) + be.astype(jnp.float32)).reshape(
        T, N_GROUPS, EXPERTS_PER_GROUP)
    e_sel = jnp.take_along_axis(e_logits, g_idx[:, :, None], axis=1)[:, 0]
    e_prob = jax.nn.softmax(e_sel, axis=-1)
    e_w, e_idx = lax.top_k(e_prob, TOP_K_IN_GROUP)
    e_w = e_w / jnp.sum(e_w, axis=-1, keepdims=True)
    gate = (g_w * e_w).astype(h.dtype)
    expert_id = (g_idx * EXPERTS_PER_GROUP + e_idx).astype(jnp.int32)

    A = T * TOP_K_IN_GROUP
    flat_e = expert_id.reshape(-1)
    flat_tok = jnp.repeat(jnp.arange(T, dtype=jnp.int32), TOP_K_IN_GROUP)
    flat_g = gate.reshape(-1)
    order = jnp.argsort(flat_e)
    se = flat_e[order]
    counts = jnp.bincount(flat_e, length=N_EXPERTS).astype(jnp.int32)
    padded = ((counts + MOE_BLOCK - 1) // MOE_BLOCK) * MOE_BLOCK
    pad_end = jnp.cumsum(padded)
    pad_start = pad_end - padded
    cnt_start = jnp.cumsum(counts) - counts
    dest = pad_start[se] + jnp.arange(A, dtype=jnp.int32) - cnt_start[se]
    P = ((A + MOE_BLOCK - 1) // MOE_BLOCK) * MOE_BLOCK + N_EXPERTS * MOE_BLOCK
    nblk = P // MOE_BLOCK
    row_tok = jnp.full((P,), T, jnp.int32).at[dest].set(flat_tok[order])
    row_gate = jnp.zeros((P,), h.dtype).at[dest].set(flat_g[order])
    blk_e = jnp.minimum(
        jnp.searchsorted(pad_end, jnp.arange(nblk, dtype=jnp.int32) * MOE_BLOCK, side='right'),
        N_EXPERTS - 1).astype(jnp.int32)
    x_pad = jnp.concatenate([xf, jnp.zeros((1, D), xf.dtype)], axis=0)
    xs = x_pad[row_tok].reshape(nblk, MOE_BLOCK, D)

    def expert_block(args):
        xb, e = args
        a, b = jnp.split(xb @ w13[e], 2, axis=-1)
        return (jax.nn.silu(a) * b) @ w2[e]

    ys = lax.map(expert_block, (xs, blk_e)).reshape(P, D)
    out = jnp.zeros((T + 1, D), h.dtype).at[row_tok].add(ys * row_gate[:, None])[:T]
    return out.reshape(B, S, D)


def setup_inputs(seed: int = 0) -> dict:
    key = jax.random.key(seed)
    ks = jax.random.split(key, 32)
    f32 = jnp.float32
    L = DEPTH

    def nrm(k, shape, scale):
        return jax.random.normal(k, shape, f32) * scale

    def gain(k, shape):
        return 1.0 + 0.01 * jax.random.normal(k, shape, f32)

    x = nrm(ks[0], (BATCH, SEQ, D_MODEL), 1.0)
    positions = (jnp.arange(SEQ, dtype=jnp.int32)[None, :]
                 + jax.random.randint(ks[1], (BATCH, 1), 0, MAX_POS_OFFSET, dtype=jnp.int32))
    return {
        'x': x,
        'positions': positions,
        'attn_norm_g': gain(ks[2], (L, D_MODEL)),
        'w_in': nrm(ks[3], (L, D_MODEL, IN_WIDTH), D_MODEL ** -0.5),
        'conv_dw_w': nrm(ks[4], (L, CONV_WIDTH, CONV_CH), CONV_WIDTH ** -0.5),
        'conv_dw_b': nrm(ks[5], (L, CONV_CH), 0.02),
        'conv_ln_g': gain(ks[6], (L, CONV_CH)),
        'conv_ln_b': nrm(ks[7], (L, CONV_CH), 0.02),
        'conv_pw_w': nrm(ks[8], (L, CONV_CH, CONV_CH), CONV_CH ** -0.5),
        'conv_out_g': gain(ks[9], (L, CONV_CH)),
        'swa_sinks': nrm(ks[10], (L, SWA_Q_HEADS), 0.5),
        'swa_out_g': gain(ks[11], (L, SWA_WIDTH)),
        'diff_lambda_q1': nrm(ks[12], (L, HEAD_DIM), 0.1),
        'diff_lambda_k1': nrm(ks[13], (L, HEAD_DIM), 0.1),
        'diff_lambda_q2': nrm(ks[14], (L, HEAD_DIM), 0.1),
        'diff_lambda_k2': nrm(ks[15], (L, HEAD_DIM), 0.1),
        'diff_subln_g': gain(ks[16], (L, 2 * HEAD_DIM)),
        'w_out': nrm(ks[17], (L, MIX_WIDTH, D_MODEL), MIX_WIDTH ** -0.5),
        'ffn_norm_g': gain(ks[18], (L, D_MODEL)),
        'router_group_w': nrm(ks[19], (L, D_MODEL, N_GROUPS), D_MODEL ** -0.5),
        'router_group_b': nrm(ks[20], (L, N_GROUPS), 0.01),
        'router_expert_w': nrm(ks[21], (L, D_MODEL, N_EXPERTS), D_MODEL ** -0.5),
        'router_expert_b': nrm(ks[22], (L, N_EXPERTS), 0.01),
        'moe_w13': nrm(ks[23], (L, N_EXPERTS, D_MODEL, 2 * EXPERT_FF), D_MODEL ** -0.5),
        'moe_w2': nrm(ks[24], (L, N_EXPERTS, EXPERT_FF, D_MODEL), EXPERT_FF ** -0.5),
        'final_norm_g': gain(ks[25], (D_MODEL,)),
    }


def reference(x, positions, attn_norm_g, w_in, conv_dw_w, conv_dw_b, conv_ln_g, conv_ln_b,
              conv_pw_w, conv_out_g, swa_sinks, swa_out_g, diff_lambda_q1, diff_lambda_k1,
              diff_lambda_q2, diff_lambda_k2, diff_subln_g, w_out, ffn_norm_g,
              router_group_w, router_group_b, router_expert_w, router_expert_b,
              moe_w13, moe_w2, final_norm_g):
    B, S, _ = x.shape
    cos, sin = rope_tables(positions, HEAD_DIM)
    skv = SWA_KV_HEADS * HEAD_DIM
    offs = [2 * CONV_CH]
    for w in (SWA_WIDTH, skv, skv, DIFF_WIDTH, DIFF_WIDTH):
        offs.append(offs[-1] + w)

    for i in range(DEPTH):
        h = rms_norm(x, attn_norm_g[i])
        proj = h @ w_in[i]
        u_c, q_s, k_s, v_s, q_d, k_d, v_d = jnp.split(proj, offs, axis=-1)

        y_conv = conv_module(u_c, conv_dw_w[i], conv_dw_b[i], conv_ln_g[i], conv_ln_b[i],
                             conv_pw_w[i])

        q_s = apply_rope(q_s.reshape(B, S, SWA_Q_HEADS, HEAD_DIM), cos, sin)
        k_s = apply_rope(k_s.reshape(B, S, SWA_KV_HEADS, HEAD_DIM), cos, sin)
        v_s = v_s.reshape(B, S, SWA_KV_HEADS, HEAD_DIM)
        y_swa = swa_attention(q_s, k_s, v_s, swa_sinks[i])

        lambda_init = 0.8 - 0.6 * float(np.exp(-0.3 * i))
        lam = (jnp.exp(jnp.sum(diff_lambda_q1[i].astype(jnp.float32) * diff_lambda_k1[i].astype(jnp.float32)))
               - jnp.exp(jnp.sum(diff_lambda_q2[i].astype(jnp.float32) * diff_lambda_k2[i].astype(jnp.float32)))
               + lambda_init)
        q_d = apply_rope(q_d.reshape(B, S, 2 * DIFF_HEADS, HEAD_DIM), cos, sin).reshape(
            B, S, DIFF_HEADS, 2, HEAD_DIM)
        k_d = apply_rope(k_d.reshape(B, S, 2 * DIFF_HEADS, HEAD_DIM), cos, sin).reshape(
            B, S, DIFF_HEADS, 2, HEAD_DIM)
        v_d = v_d.reshape(B, S, DIFF_HEADS, 2 * HEAD_DIM)
        o_d = diff_attention(q_d, k_d, v_d, lam)
        y_diff = (rms_norm(o_d, diff_subln_g[i], LN_EPS) * (1.0 - lambda_init)).reshape(
            B, S, DIFF_WIDTH)

        mix = jnp.concatenate([rms_norm(y_conv, conv_out_g[i]),
                               rms_norm(y_swa, swa_out_g[i]),
                               y_diff], axis=-1)
        x = x + mix @ w_out[i]

        h = rms_norm(x, ffn_norm_g[i])
        x = x + hier_moe(h, router_group_w[i], router_group_b[i], router_expert_w[i],
                         router_expert_b[i], moe_w13[i], moe_w2[i])

    return rms_norm(x, final_norm_g)
```

```python
import functools

import numpy as np
import jax
import jax.numpy as jnp
from jax import lax
from jax.experimental import pallas as pl
from jax.experimental.pallas import tpu as pltpu

F32 = jnp.float32
BF16 = jnp.bfloat16
I32 = jnp.int32
U32 = jnp.uint32

D_MODEL = 2048
HEAD_DIM = 64
CONV_CH = 512
CONV_WIDTH = 31
SWA_Q_HEADS = 12
SWA_KV_HEADS = 4
SWA_GROUP = SWA_Q_HEADS // SWA_KV_HEADS
SWA_WINDOW = 128
SWA_WIDTH = SWA_Q_HEADS * HEAD_DIM
SWA_KV_WIDTH = SWA_KV_HEADS * HEAD_DIM
DIFF_HEADS = 6
DIFF_WIDTH = DIFF_HEADS * 2 * HEAD_DIM
IN_WIDTH = 2 * CONV_CH + SWA_WIDTH + 2 * SWA_KV_WIDTH + 3 * DIFF_WIDTH
ROPE_THETA = 10000.0
N_GROUPS = 8
EXPERTS_PER_GROUP = 8
N_EXPERTS = N_GROUPS * EXPERTS_PER_GROUP
EXPERT_FF = 512
RMS_EPS = 1e-6
LN_EPS = 1e-5
NEG_INF = -1e30

LANES = 128
SUBLANES = 8
VMEM_LIMIT_BYTES = 56 * 1024 * 1024

INPROJ_ROWS = 512
CONV_ROWS = 256
CONV_HALO = 32
DIFF_Q_ROWS = 256
DIFF_K_ROWS = 256
OUTPROJ_ROWS = 512
MOE_ROWS = 128
GATHER_ROWS = 256
COMBINE_ROWS = 256
ROUTE_LANES = 128


def _params(*sem):
    return pltpu.CompilerParams(dimension_semantics=sem, vmem_limit_bytes=VMEM_LIMIT_BYTES)


def _resident(shape):
    nd = len(shape)
    return pl.BlockSpec(shape, lambda *_: (0,) * nd, pipeline_mode=pl.Buffered(1))


def _rope(r, cos, sin_signed, first_half):
    outs = []
    for k in range(r.shape[1] // LANES):
        seg = r[:, k * LANES:(k + 1) * LANES]
        partner = jnp.where(first_half,
                            pltpu.roll(seg, LANES - HEAD_DIM // 2, 1),
                            pltpu.roll(seg, HEAD_DIM // 2, 1))
        outs.append(seg * cos + partner * sin_signed)
    return outs[0] if len(outs) == 1 else jnp.concatenate(outs, axis=1)


def _inproj_kernel(x_ref, g_ref, w_ref, cos_ref, sin_ref,
                   z_ref, qs_ref, ks_ref, vs_ref, qd_ref, kd_ref, vd_ref):
    x = x_ref[...]
    ms = jnp.mean(x * x, axis=-1, keepdims=True)
    h = (x * lax.rsqrt(ms + RMS_EPS) * g_ref[...]).astype(BF16)
    cos = cos_ref[...]
    sin_signed = sin_ref[...]
    lane = lax.broadcasted_iota(I32, cos.shape, 1)
    first_half = (lane & (HEAD_DIM // 2)) == 0
    nc = 2 * LANES
    q_scale = HEAD_DIM ** -0.5

    def proj(c0):
        return jnp.dot(h, w_ref[:, c0:c0 + nc], preferred_element_type=F32)

    for j in range(CONV_CH // nc):
        a = proj(j * nc)
        gate = proj(CONV_CH + j * nc)
        z_ref[:, j * nc:(j + 1) * nc] = a * jax.nn.sigmoid(gate)
    base = 2 * CONV_CH
    for j in range(SWA_WIDTH // nc):
        r = _rope(proj(base + j * nc), cos, sin_signed, first_half)
        qs_ref[:, j * nc:(j + 1) * nc] = (r * q_scale).astype(BF16)
    base += SWA_WIDTH
    for j in range(SWA_KV_WIDTH // nc):
        r = _rope(proj(base + j * nc), cos, sin_signed, first_half)
        ks_ref[:, j * nc:(j + 1) * nc] = r.astype(BF16)
    base += SWA_KV_WIDTH
    for j in range(SWA_KV_WIDTH // nc):
        vs_ref[:, j * nc:(j + 1) * nc] = proj(base + j * nc).astype(BF16)
    base += SWA_KV_WIDTH
    for j in range(DIFF_WIDTH // nc):
        r = _rope(proj(base + j * nc), cos, sin_signed, first_half)
        qd_ref[:, j * nc:(j + 1) * nc] = (r * q_scale).astype(BF16)
    base += DIFF_WIDTH
    for j in range(DIFF_WIDTH // nc):
        r = _rope(proj(base + j * nc), cos, sin_signed, first_half)
        kd_ref[:, j * nc:(j + 1) * nc] = r.astype(BF16)
    base += DIFF_WIDTH
    for j in range(DIFF_WIDTH // nc):
        vd_ref[:, j * nc:(j + 1) * nc] = proj(base + j * nc).astype(BF16)


def _inproj(x2, g, w_bf, cos128, sin128):
    t = x2.shape[0]
    tm = min(INPROJ_ROWS, t)
    row = lambda w: pl.BlockSpec((tm, w), lambda i: (i, 0))
    widths = (CONV_CH, SWA_WIDTH, SWA_KV_WIDTH, SWA_KV_WIDTH, DIFF_WIDTH, DIFF_WIDTH, DIFF_WIDTH)
    dtypes = (F32,) + (BF16,) * 6
    return pl.pallas_call(
        _inproj_kernel,
        grid=(t // tm,),
        in_specs=[row(D_MODEL), _resident((1, D_MODEL)), _resident((D_MODEL, IN_WIDTH)),
                  row(LANES), row(LANES)],
        out_specs=[row(w) for w in widths],
        out_shape=[jax.ShapeDtypeStruct((t, w), dt) for w, dt in zip(widths, dtypes)],
        compiler_params=_params("parallel"),
        name="inproj",
    )(x2, g, w_bf, cos128, sin128)


def _conv_kernel(z_ref, halo_ref, dww_ref, dwb_ref, lng_ref, lnb_ref, pw_ref, og_ref,
                 o_ref, buf_ref, y_ref):
    ts = z_ref.shape[1]
    i = pl.program_id(1)
    buf_ref[0:CONV_HALO, :] = jnp.where(i > 0, halo_ref[0], 0.0)
    buf_ref[CONV_HALO:CONV_HALO + ts, :] = z_ref[0]
    rows = 64
    first = CONV_HALO - (CONV_WIDTH - 1)
    for c in range(CONV_CH // LANES):
        cs = slice(c * LANES, (c + 1) * LANES)
        for r in range(ts // rows):
            acc = jnp.broadcast_to(dwb_ref[:, cs], (rows, LANES))
            for j in range(CONV_WIDTH):
                s0 = first + j + r * rows
                acc = acc + buf_ref[s0:s0 + rows, cs] * dww_ref[j:j + 1, cs]
            y_ref[r * rows:(r + 1) * rows, cs] = acc
    y = y_ref[...]
    mu = jnp.mean(y, axis=-1, keepdims=True)
    yc = y - mu
    var = jnp.mean(yc * yc, axis=-1, keepdims=True)
    yn = yc * lax.rsqrt(var + LN_EPS) * lng_ref[...] + lnb_ref[...]
    act = yn * jax.nn.sigmoid(yn)
    p = jnp.dot(act.astype(BF16), pw_ref[...], preferred_element_type=F32)
    ms = jnp.mean(p * p, axis=-1, keepdims=True)
    o_ref[0] = (p * lax.rsqrt(ms + RMS_EPS) * og_ref[...]).astype(BF16)


def _conv(z3, dw_w, dw_b, ln_g, ln_b, pw_bf, out_g):
    b, s, _ = z3.shape
    ts = min(CONV_ROWS, s)
    hb = ts // CONV_HALO
    return pl.pallas_call(
        _conv_kernel,
        grid=(b, s // ts),
        in_specs=[pl.BlockSpec((1, ts, CONV_CH), lambda bi, i: (bi, i, 0)),
                  pl.BlockSpec((1, CONV_HALO, CONV_CH), lambda bi, i: (bi, jnp.maximum(i * hb - 1, 0), 0)),
                  _resident((CONV_WIDTH, CONV_CH)), _resident((1, CONV_CH)), _resident((1, CONV_CH)),
                  _resident((1, CONV_CH)), _resident((CONV_CH, CONV_CH)), _resident((1, CONV_CH))],
        out_specs=pl.BlockSpec((1, ts, CONV_CH), lambda bi, i: (bi, i, 0)),
        out_shape=jax.ShapeDtypeStruct((b, s, CONV_CH), BF16),
        scratch_shapes=[pltpu.VMEM((CONV_HALO + ts, CONV_CH), F32), pltpu.VMEM((ts, CONV_CH), F32)],
        compiler_params=_params("parallel", "arbitrary"),
        name="conv",
    )(z3, z3, dw_w, dw_b, ln_g, ln_b, pw_bf, out_g)


def _swa_kernel(q_ref, kc_ref, kp_ref, vc_ref, vp_ref, sink_ref, g_ref, o_ref, acc_ref):
    w = SWA_WINDOW
    n = pl.program_id(1)
    qi = lax.broadcasted_iota(I32, (w, w), 0)
    kj = lax.broadcasted_iota(I32, (w, w), 1)
    cur_ok = kj <= qi
    prev_ok = (kj > qi) & (n > 0)
    contract = (((1,), (1,)), ((), ()))
    ss = jnp.zeros((w, 1), F32)
    for hk in range(SWA_KV_HEADS):
        ks = slice(hk * HEAD_DIM, (hk + 1) * HEAD_DIM)
        kc, kp, vc, vp = kc_ref[0, :, ks], kp_ref[0, :, ks], vc_ref[0, :, ks], vp_ref[0, :, ks]
        for gq in range(SWA_GROUP):
            hq = hk * SWA_GROUP + gq
            qs = slice(hq * HEAD_DIM, (hq + 1) * HEAD_DIM)
            q = q_ref[0, :, qs]
            sc = jnp.where(cur_ok, lax.dot_general(q, kc, contract, preferred_element_type=F32), NEG_INF)
            sp = jnp.where(prev_ok, lax.dot_general(q, kp, contract, preferred_element_type=F32), NEG_INF)
            sink = sink_ref[:, hq:hq + 1]
            m = jnp.maximum(jnp.maximum(jnp.max(sc, axis=-1, keepdims=True),
                                        jnp.max(sp, axis=-1, keepdims=True)), sink)
            ec = jnp.where(cur_ok, jnp.exp(sc - m), 0.0)
            ep = jnp.where(prev_ok, jnp.exp(sp - m), 0.0)
            den = (jnp.sum(ec, axis=-1, keepdims=True) + jnp.sum(ep, axis=-1, keepdims=True)
                   + jnp.exp(sink - m))
            inv = 1.0 / den
            o = (jnp.dot((ec * inv).astype(BF16), vc, preferred_element_type=F32)
                 + jnp.dot((ep * inv).astype(BF16), vp, preferred_element_type=F32))
            ss = ss + jnp.sum(o * o, axis=-1, keepdims=True)
            acc_ref[:, qs] = o
    scale = lax.rsqrt(ss * (1.0 / SWA_WIDTH) + RMS_EPS)
    o_ref[0] = (acc_ref[...] * scale * g_ref[...]).astype(BF16)


def _swa(q3, k3, v3, sinks, out_g):
    b, s, _ = q3.shape
    w = SWA_WINDOW
    cur = lambda width: pl.BlockSpec((1, w, width), lambda bi, n: (bi, n, 0))
    prev = lambda width: pl.BlockSpec((1, w, width), lambda bi, n: (bi, jnp.maximum(n - 1, 0), 0))
    return pl.pallas_call(
        _swa_kernel,
        grid=(b, s // w),
        in_specs=[cur(SWA_WIDTH), cur(SWA_KV_WIDTH), prev(SWA_KV_WIDTH), cur(SWA_KV_WIDTH),
                  prev(SWA_KV_WIDTH), _resident((1, SWA_Q_HEADS)), _resident((1, SWA_WIDTH))],
        out_specs=cur(SWA_WIDTH),
        out_shape=jax.ShapeDtypeStruct((b, s, SWA_WIDTH), BF16),
        scratch_shapes=[pltpu.VMEM((w, SWA_WIDTH), F32)],
        compiler_params=_params("parallel", "arbitrary"),
        name="swa",
    )(q3, k3, k3, v3, v3, sinks, out_g)


def _diff_kernel(q_ref, k_ref, v_ref, lq1_ref, lk1_ref, lq2_ref, lk2_ref, g_ref, o_ref,
                 m_ref, l_ref, acc_ref, *, lambda_init):
    tq = q_ref.shape[1]
    tk = DIFF_K_ROWS if k_ref.shape[1] >= DIFF_K_ROWS else k_ref.shape[1]
    i = pl.program_id(2)
    contract = (((1,), (1,)), ((), ()))
    d = HEAD_DIM
    m_ref[...] = jnp.full(m_ref.shape, NEG_INF, F32)
    l_ref[...] = jnp.zeros(l_ref.shape, F32)
    acc_ref[...] = jnp.zeros(acc_ref.shape, F32)

    def step(j, masked):
        k0 = pl.multiple_of(j * tk, tk)
        kblk = k_ref[0, pl.ds(k0, tk), :]
        vblk = v_ref[0, pl.ds(k0, tk), :]
        if masked:
            qpos = i * tq + lax.broadcasted_iota(I32, (tq, tk), 0)
            kpos = k0 + lax.broadcasted_iota(I32, (tq, tk), 1)
            ok = kpos <= qpos
        for c in range(2):
            cs = slice(c * d, (c + 1) * d)
            s = lax.dot_general(q_ref[0, :, cs], kblk[:, cs], contract, preferred_element_type=F32)
            if masked:
                s = jnp.where(ok, s, NEG_INF)
            m_old = m_ref[c]
            m_new = jnp.maximum(m_old, jnp.max(s, axis=-1, keepdims=True))
            alpha = jnp.exp(m_old - m_new)
            p = jnp.exp(s - m_new)
            l_ref[c] = alpha * l_ref[c] + jnp.sum(p, axis=-1, keepdims=True)
            acc_ref[c] = alpha * acc_ref[c] + jnp.dot(p.astype(BF16), vblk, preferred_element_type=F32)
            m_ref[c] = m_new

    n_full = (i * tq) // tk
    n_diag = tq // tk

    def body(j, carry):
        step(j, False)
        return carry

    lax.fori_loop(0, n_full, body, 0)
    for jd in range(n_diag):
        step(n_full + jd, True)

    lam = (jnp.exp(jnp.sum(lq1_ref[...] * lk1_ref[...], axis=-1, keepdims=True))
           - jnp.exp(jnp.sum(lq2_ref[...] * lk2_ref[...], axis=-1, keepdims=True)) + lambda_init)
    o = acc_ref[0] / l_ref[0] - lam * (acc_ref[1] / l_ref[1])
    ms = jnp.mean(o * o, axis=-1, keepdims=True)
    o_ref[0] = (o * lax.rsqrt(ms + LN_EPS) * g_ref[...] * (1.0 - lambda_init)).astype(BF16)


def _diff(q3, k3, v3, lq1, lk1, lq2, lk2, subln_g, lambda_init):
    b, s, _ = q3.shape
    tq = min(DIFF_Q_ROWS, s)
    hw = 2 * HEAD_DIM
    qspec = pl.BlockSpec((1, tq, hw), lambda bi, h, i: (bi, i, h))
    kvspec = pl.BlockSpec((1, s, hw), lambda bi, h, i: (bi, 0, h))
    vec = _resident((1, HEAD_DIM))
    return pl.pallas_call(
        functools.partial(_diff_kernel, lambda_init=lambda_init),
        grid=(b, DIFF_HEADS, s // tq),
        in_specs=[qspec, kvspec, kvspec, vec, vec, vec, vec, _resident((1, hw))],
        out_specs=qspec,
        out_shape=jax.ShapeDtypeStruct((b, s, DIFF_WIDTH), BF16),
        scratch_shapes=[pltpu.VMEM((2, tq, 1), F32), pltpu.VMEM((2, tq, 1), F32),
                        pltpu.VMEM((2, tq, hw), F32)],
        compiler_params=_params("parallel", "parallel", "arbitrary"),
        name="diff",
    )(q3, k3, v3, lq1, lk1, lq2, lk2, subln_g)


def _bits(v):
    return lax.bitcast_convert_type(v, U32)


def _route(logits):
    lane = lax.broadcasted_iota(I32, logits.shape, 1)
    big = jnp.int32(ROUTE_LANES)
    is_group = lane < N_GROUPS
    gmax = jnp.max(jnp.where(is_group, logits, NEG_INF), axis=-1, keepdims=True)
    g_idx = jnp.min(jnp.where(is_group & (logits == gmax), lane, big), axis=-1, keepdims=True)
    g_w = 1.0 / jnp.sum(jnp.where(is_group, jnp.exp(logits - gmax), 0.0), axis=-1, keepdims=True)
    lo = N_GROUPS + g_idx * EXPERTS_PER_GROUP
    in_sel = (lane >= lo) & (lane < lo + EXPERTS_PER_GROUP)
    e1 = jnp.max(jnp.where(in_sel, logits, NEG_INF), axis=-1, keepdims=True)
    i1 = jnp.min(jnp.where(in_sel & (logits == e1), lane, big), axis=-1, keepdims=True)
    rest = in_sel & (lane != i1)
    e2 = jnp.max(jnp.where(rest, logits, NEG_INF), axis=-1, keepdims=True)
    i2 = jnp.min(jnp.where(rest & (logits == e2), lane, big), axis=-1, keepdims=True)
    t = jnp.exp(e2 - e1)
    w1 = 1.0 / (1.0 + t)
    w2 = t * w1
    out = jnp.where(lane == 0, (i1 - N_GROUPS).astype(F32),
          jnp.where(lane == 1, (i2 - N_GROUPS).astype(F32),
          jnp.where(lane == 2, g_w * w1,
          jnp.where(lane == 3, g_w * w2, 0.0))))
    return out


def _outproj_kernel(x_ref, c_ref, s_ref, d_ref, w_ref, g_ref, wr_ref, br_ref,
                    xo_ref, hp_ref, route_ref):
    tm = x_ref.shape[0]
    nc = 4 * LANES
    c_in, s_in, d_in = c_ref[...], s_ref[...], d_ref[...]
    o1, o2 = CONV_CH, CONV_CH + SWA_WIDTH
    ss = jnp.zeros((tm, 1), F32)
    for j in range(D_MODEL // nc):
        cs = slice(j * nc, (j + 1) * nc)
        y = (jnp.dot(c_in, w_ref[0:o1, cs], preferred_element_type=F32)
             + jnp.dot(s_in, w_ref[o1:o2, cs], preferred_element_type=F32)
             + jnp.dot(d_in, w_ref[o2:D_MODEL, cs], preferred_element_type=F32))
        xn = x_ref[:, cs] + y
        xo_ref[:, cs] = xn
        ss = ss + jnp.sum(xn * xn, axis=-1, keepdims=True)
    scale = lax.rsqrt(ss * (1.0 / D_MODEL) + RMS_EPS)
    half = D_MODEL // 2
    h_lo = xo_ref[:, 0:half] * scale * g_ref[:, 0:half]
    h_hi = xo_ref[:, half:D_MODEL] * scale * g_ref[:, half:D_MODEL]
    b_lo, b_hi = h_lo.astype(BF16), h_hi.astype(BF16)
    hp_ref[...] = (_bits(b_lo.astype(F32)) >> 16) | (_bits(b_hi.astype(F32)) & jnp.uint32(0xFFFF0000))
    r_lo = (h_lo - b_lo.astype(F32)).astype(BF16)
    r_hi = (h_hi - b_hi.astype(F32)).astype(BF16)
    acc = (jnp.dot(b_lo, wr_ref[0:half, :], preferred_element_type=F32)
           + jnp.dot(b_hi, wr_ref[half:D_MODEL, :], preferred_element_type=F32)
           + jnp.dot(r_lo, wr_ref[0:half, :], preferred_element_type=F32)
           + jnp.dot(r_hi, wr_ref[half:D_MODEL, :], preferred_element_type=F32))
    logits = acc[:, 0:ROUTE_LANES] + acc[:, ROUTE_LANES:2 * ROUTE_LANES] + br_ref[...]
    route_ref[...] = _route(logits)


def _outproj(x2, conv_o, swa_o, diff_o, w_bf, ffn_g, wr_bf, br):
    t = x2.shape[0]
    tm = min(OUTPROJ_ROWS, t)
    row = lambda w: pl.BlockSpec((tm, w), lambda i: (i, 0))
    return pl.pallas_call(
        _outproj_kernel,
        grid=(t // tm,),
        in_specs=[row(D_MODEL), row(CONV_CH), row(SWA_WIDTH), row(DIFF_WIDTH),
                  _resident((D_MODEL, D_MODEL)), _resident((1, D_MODEL)),
                  _resident((D_MODEL, 2 * ROUTE_LANES)), _resident((1, ROUTE_LANES))],
        out_specs=[row(D_MODEL), row(D_MODEL // 2), row(ROUTE_LANES)],
        out_shape=[jax.ShapeDtypeStruct((t, D_MODEL), F32),
                   jax.ShapeDtypeStruct((t, D_MODEL // 2), U32),
                   jax.ShapeDtypeStruct((t, ROUTE_LANES), F32)],
        compiler_params=_params("parallel"),
        name="outproj",
    )(x2, conv_o, swa_o, diff_o, w_bf, ffn_g, wr_bf, br)


def _gather_kernel(nrows_ref, idx_ref, src_ref, o_ref, sem):
    tg = o_ref.shape[0]
    i = pl.program_id(0)
    used = i * tg < nrows_ref[0]

    def row_copy(r):
        return pltpu.make_async_copy(src_ref.at[pl.ds(idx_ref[0, 0, r], 1), :],
                                     o_ref.at[pl.ds(r, 1), :], sem)

    @pl.when(used)
    def _():
        def start(r, c):
            row_copy(r).start()
            return c
        lax.fori_loop(0, tg, start, 0)

        def wait(r, c):
            row_copy(r).wait()
            return c
        lax.fori_loop(0, tg, wait, 0)

    @pl.when(jnp.logical_not(used))
    def _():
        o_ref[...] = jnp.zeros(o_ref.shape, o_ref.dtype)


def _gather_rows(nrows, idx3, src):
    nt, _, tg = idx3.shape
    width = src.shape[1]
    return pl.pallas_call(
        _gather_kernel,
        grid_spec=pltpu.PrefetchScalarGridSpec(
            num_scalar_prefetch=1,
            grid=(nt,),
            in_specs=[pl.BlockSpec((1, 1, tg), lambda i, n: (i, 0, 0), memory_space=pltpu.SMEM),
                      pl.BlockSpec(memory_space=pl.ANY)],
            out_specs=pl.BlockSpec((tg, width), lambda i, n: (i, 0)),
            scratch_shapes=[pltpu.SemaphoreType.DMA],
        ),
        out_shape=jax.ShapeDtypeStruct((nt * tg, width), src.dtype),
        compiler_params=_params("arbitrary"),
        name="gather",
    )(nrows, idx3, src)


def _experts_kernel(be_ref, bfirst_ref, bslot_ref, bnext_ref,
                    xs_ref, w13_hbm, w2_hbm, ys_ref,
                    w13_buf, w2_buf, w13_bf, w2_bf, sem13, sem2):
    i = pl.program_id(0)
    e = be_ref[i]
    valid = e >= 0
    slot = bslot_ref[i]

    def w_copies(expert, s):
        return (pltpu.make_async_copy(w13_hbm.at[expert], w13_buf.at[s], sem13.at[s]),
                pltpu.make_async_copy(w2_hbm.at[expert], w2_buf.at[s], sem2.at[s]))

    @pl.when(valid & (bfirst_ref[i] == 1))
    def _():
        @pl.when(i == 0)
        def _():
            for cp in w_copies(e, slot):
                cp.start()
        for cp in w_copies(e, slot):
            cp.wait()
        nxt = bnext_ref[i]

        @pl.when(nxt >= 0)
        def _():
            for cp in w_copies(nxt, 1 - slot):
                cp.start()
        w13_bf[...] = w13_buf[slot].astype(BF16)
        w2_bf[...] = w2_buf[slot].astype(BF16)

    @pl.when(valid)
    def _():
        half = D_MODEL // 2
        u = xs_ref[...]
        x_lo = lax.bitcast_convert_type(u << 16, F32).astype(BF16)
        x_hi = lax.bitcast_convert_type(u & jnp.uint32(0xFFFF0000), F32).astype(BF16)
        h = (jnp.dot(x_lo, w13_bf[0:half, :], preferred_element_type=F32)
             + jnp.dot(x_hi, w13_bf[half:D_MODEL, :], preferred_element_type=F32))
        a, b = h[:, 0:EXPERT_FF], h[:, EXPERT_FF:2 * EXPERT_FF]
        act = (a * jax.nn.sigmoid(a) * b).astype(BF16)
        ys_ref[...] = jnp.dot(act, w2_bf[...], preferred_element_type=F32)

    @pl.when(jnp.logical_not(valid))
    def _():
        ys_ref[...] = jnp.zeros(ys_ref.shape, ys_ref.dtype)


def _experts(blk_e, blk_first, blk_slot, blk_next, xs, w13, w2):
    nblk = blk_e.shape[0]
    tm = xs.shape[0] // nblk
    return pl.pallas_call(
        _experts_kernel,
        grid_spec=pltpu.PrefetchScalarGridSpec(
            num_scalar_prefetch=4,
            grid=(nblk,),
            in_specs=[pl.BlockSpec((tm, D_MODEL // 2), lambda i, *_: (i, 0)),
                      pl.BlockSpec(memory_space=pl.ANY), pl.BlockSpec(memory_space=pl.ANY)],
            out_specs=pl.BlockSpec((tm, D_MODEL), lambda i, *_: (i, 0)),
            scratch_shapes=[pltpu.VMEM((2, D_MODEL, 2 * EXPERT_FF), F32),
                            pltpu.VMEM((2, EXPERT_FF, D_MODEL), F32),
                            pltpu.VMEM((D_MODEL, 2 * EXPERT_FF), BF16),
                            pltpu.VMEM((EXPERT_FF, D_MODEL), BF16),
                            pltpu.SemaphoreType.DMA((2,)), pltpu.SemaphoreType.DMA((2,))],
        ),
        out_shape=jax.ShapeDtypeStruct((nblk * tm, D_MODEL), F32),
        compiler_params=_params("arbitrary"),
        name="experts",
    )(blk_e, blk_first, blk_slot, blk_next, xs, w13, w2)


def _combine_kernel(slot_ref, x_ref, route_ref, ys_hbm, g_ref, o_ref, buf, sem, *, final_norm):
    tm = x_ref.shape[0]

    def row_copy(r):
        k = r // tm
        return pltpu.make_async_copy(ys_hbm.at[pl.ds(slot_ref[0, 0, r], 1), :],
                                     buf.at[k, pl.ds(r - k * tm, 1), :], sem)

    def start(r, c):
        row_copy(r).start()
        return c
    lax.fori_loop(0, 2 * tm, start, 0)

    def wait(r, c):
        row_copy(r).wait()
        return c
    lax.fori_loop(0, 2 * tm, wait, 0)

    route = route_ref[...]
    moe = buf[0] * route[:, 2:3] + buf[1] * route[:, 3:4]
    xn = x_ref[...] + moe
    if final_norm:
        ms = jnp.mean(xn * xn, axis=-1, keepdims=True)
        xn = xn * lax.rsqrt(ms + RMS_EPS) * g_ref[...]
    o_ref[...] = xn


def _combine(slot3, x2, route, ys, final_g, final_norm):
    t = x2.shape[0]
    nt, _, two_tm = slot3.shape
    tm = two_tm // 2
    row = lambda w: pl.BlockSpec((tm, w), lambda i: (i, 0))
    return pl.pallas_call(
        functools.partial(_combine_kernel, final_norm=final_norm),
        grid=(nt,),
        in_specs=[pl.BlockSpec((1, 1, two_tm), lambda i: (i, 0, 0), memory_space=pltpu.SMEM),
                  row(D_MODEL), row(ROUTE_LANES), pl.BlockSpec(memory_space=pl.ANY),
                  _resident((1, D_MODEL))],
        out_specs=row(D_MODEL),
        out_shape=jax.ShapeDtypeStruct((t, D_MODEL), F32),
        scratch_shapes=[pltpu.VMEM((2, tm, D_MODEL), F32), pltpu.SemaphoreType.DMA],
        compiler_params=_params("arbitrary"),
        name="combine",
    )(slot3, x2, route, ys, final_g)


def _dispatch_tables(route, t):
    tm = MOE_ROWS
    a = 2 * t
    flat_e = route[:, 0:2].astype(I32).reshape(-1)
    flat_tok = jnp.repeat(jnp.arange(t, dtype=I32), 2)
    order = jnp.argsort(flat_e)
    se = flat_e[order]
    counts = jnp.bincount(flat_e, length=N_EXPERTS).astype(I32)
    nblk_e = (counts + tm - 1) // tm
    padded = nblk_e * tm
    pad_end = jnp.cumsum(padded)
    pad_start = pad_end - padded
    cnt_start = jnp.cumsum(counts) - counts
    dest = pad_start[se] + jnp.arange(a, dtype=I32) - cnt_start[se]
    nblk = a // tm + N_EXPERTS
    p = nblk * tm
    row_tok = jnp.zeros((p,), I32).at[dest].set(flat_tok[order])
    slot = jnp.zeros((a,), I32).at[order].set(dest)
    nrows = pad_end[-1:]
    blk_row = jnp.arange(nblk, dtype=I32) * tm
    blk_e = jnp.searchsorted(pad_end, blk_row, side='right').astype(I32)
    blk_valid = blk_row < nrows[0]
    blk_e = jnp.where(blk_valid, jnp.minimum(blk_e, N_EXPERTS - 1), -1)
    blk_first = (blk_valid & (blk_row == pad_start[jnp.maximum(blk_e, 0)])).astype(I32)
    used = counts > 0
    rank = jnp.cumsum(used.astype(I32)) - 1
    ids = jnp.arange(N_EXPERTS, dtype=I32)
    cand = jnp.where(used[None, :] & (ids[None, :] > ids[:, None]), ids[None, :], N_EXPERTS)
    nxt = jnp.min(cand, axis=1)
    nxt = jnp.where(nxt >= N_EXPERTS, -1, nxt).astype(I32)
    be = jnp.maximum(blk_e, 0)
    blk_slot = (rank[be] % 2).astype(I32)
    blk_next = nxt[be]
    return row_tok, slot, nrows.astype(I32), blk_e, blk_first, blk_slot, blk_next


def _moe(x2, hp, route, w13, w2, final_g, final_norm):
    t = x2.shape[0]
    row_tok, slot, nrows, blk_e, blk_first, blk_slot, blk_next = _dispatch_tables(route, t)
    tg = GATHER_ROWS
    xs = _gather_rows(nrows, row_tok.reshape(-1, 1, tg), hp)
    ys = _experts(blk_e, blk_first, blk_slot, blk_next, xs, w13, w2)
    tc = min(COMBINE_ROWS, t)
    slot3 = slot.reshape(t // tc, tc, 2).transpose(0, 2, 1).reshape(t // tc, 1, 2 * tc)
    return _combine(slot3, x2, route, ys, final_g, final_norm)


def _rope_tables(positions):
    inv_freq = ROPE_THETA ** (-jnp.arange(0, HEAD_DIM, 2, dtype=F32) / HEAD_DIM)
    ang = positions.astype(F32)[..., None] * inv_freq
    c, s = jnp.cos(ang), jnp.sin(ang)
    t = positions.size
    cos128 = jnp.concatenate([c, c, c, c], axis=-1).reshape(t, LANES)
    sin128 = jnp.concatenate([-s, s, -s, s], axis=-1).reshape(t, LANES)
    return cos128, sin128


def _router_weights(wg, we, bg, be):
    w = jnp.zeros((D_MODEL, ROUTE_LANES), F32).at[:, 0:N_GROUPS].set(wg)
    w = w.at[:, N_GROUPS:N_GROUPS + N_EXPERTS].set(we)
    head = w.astype(BF16)
    tail = (w - head.astype(F32)).astype(BF16)
    bias = jnp.zeros((1, ROUTE_LANES), F32).at[0, 0:N_GROUPS].set(bg)
    bias = bias.at[0, N_GROUPS:N_GROUPS + N_EXPERTS].set(be)
    return jnp.concatenate([head, tail], axis=1), bias


def kernel(x, positions, attn_norm_g, w_in, conv_dw_w, conv_dw_b, conv_ln_g, conv_ln_b, conv_pw_w, conv_out_g, swa_sinks, swa_out_g, diff_lambda_q1, diff_lambda_k1, diff_lambda_q2, diff_lambda_k2, diff_subln_g, w_out, ffn_norm_g, router_group_w, router_group_b, router_expert_w, router_expert_b, moe_w13, moe_w2, final_norm_g):
    b, s, d = x.shape
    t = b * s
    depth = w_in.shape[0]
    cos128, sin128 = _rope_tables(positions)
    x2 = x.reshape(t, d)
    row = lambda v: v.reshape(1, -1)
    for i in range(depth):
        lambda_init = 0.8 - 0.6 * float(np.exp(-0.3 * i))
        z, qs, ks, vs, qd, kd, vd = _inproj(x2, row(attn_norm_g[i]), w_in[i].astype(BF16), cos128, sin128)
        conv_o = _conv(z.reshape(b, s, CONV_CH), conv_dw_w[i], row(conv_dw_b[i]), row(conv_ln_g[i]),
                       row(conv_ln_b[i]), conv_pw_w[i].astype(BF16), row(conv_out_g[i]))
        swa_o = _swa(qs.reshape(b, s, -1), ks.reshape(b, s, -1), vs.reshape(b, s, -1),
                     row(swa_sinks[i]), row(swa_out_g[i]))
        diff_o = _diff(qd.reshape(b, s, -1), kd.reshape(b, s, -1), vd.reshape(b, s, -1),
                       row(diff_lambda_q1[i]), row(diff_lambda_k1[i]), row(diff_lambda_q2[i]),
                       row(diff_lambda_k2[i]), row(diff_subln_g[i]), lambda_init)
        wr, br = _router_weights(router_group_w[i], router_expert_w[i], router_group_b[i], router_expert_b[i])
        x2, hp, route = _outproj(x2, conv_o.reshape(t, -1), swa_o.reshape(t, -1), diff_o.reshape(t, -1),
                                 w_out[i].astype(BF16), row(ffn_norm_g[i]), wr, br)
        x2 = _moe(x2, hp, route, moe_w13[i], moe_w2[i], row(final_norm_g), i == depth - 1)
    return x2.reshape(b, s, d)
```

```python
import functools

import numpy as np
import jax
import jax.numpy as jnp
from jax import lax
from jax.experimental import pallas as pl
from jax.experimental.pallas import tpu as pltpu

F32 = jnp.float32
BF16 = jnp.bfloat16
I32 = jnp.int32
U32 = jnp.uint32

D_MODEL = 2048
HEAD_DIM = 64
CONV_CH = 512
CONV_WIDTH = 31
SWA_Q_HEADS = 12
SWA_KV_HEADS = 4
SWA_GROUP = SWA_Q_HEADS // SWA_KV_HEADS
SWA_WINDOW = 128
SWA_WIDTH = SWA_Q_HEADS * HEAD_DIM
SWA_KV_WIDTH = SWA_KV_HEADS * HEAD_DIM
DIFF_HEADS = 6
DIFF_WIDTH = DIFF_HEADS * 2 * HEAD_DIM
IN_WIDTH = 2 * CONV_CH + SWA_WIDTH + 2 * SWA_KV_WIDTH + 3 * DIFF_WIDTH
ROPE_THETA = 10000.0
N_GROUPS = 8
EXPERTS_PER_GROUP = 8
N_EXPERTS = N_GROUPS * EXPERTS_PER_GROUP
EXPERT_FF = 512
RMS_EPS = 1e-6
LN_EPS = 1e-5
NEG_INF = -1e30

LANES = 128
SUBLANES = 8
VMEM_LIMIT_BYTES = 56 * 1024 * 1024

INPROJ_ROWS = 512
CONV_ROWS = 256
CONV_HALO = 32
DIFF_Q_ROWS = 512
DIFF_K_ROWS = 512
OUTPROJ_ROWS = 512
MOE_ROWS = 128
DISPATCH_ROWS = 256
COMBINE_ROWS = 256
TOKEN_TILE_ROWS = D_MODEL // 2 // LANES
ROUTE_LANES = 128


def _params(*sem):
    return pltpu.CompilerParams(dimension_semantics=sem, vmem_limit_bytes=VMEM_LIMIT_BYTES)


def _resident(shape):
    nd = len(shape)
    return pl.BlockSpec(shape, lambda *_: (0,) * nd, pipeline_mode=pl.Buffered(1))


def _rope(r, cos, sin_signed, first_half):
    outs = []
    for k in range(r.shape[1] // LANES):
        seg = r[:, k * LANES:(k + 1) * LANES]
        partner = jnp.where(first_half,
                            pltpu.roll(seg, LANES - HEAD_DIM // 2, 1),
                            pltpu.roll(seg, HEAD_DIM // 2, 1))
        outs.append(seg * cos + partner * sin_signed)
    return outs[0] if len(outs) == 1 else jnp.concatenate(outs, axis=1)


def _dup_heads(r, low_head):
    outs = []
    for k in range(r.shape[1] // LANES):
        seg = r[:, k * LANES:(k + 1) * LANES]
        rot = pltpu.roll(seg, HEAD_DIM, 1)
        outs.append(jnp.where(low_head, seg, rot))
        outs.append(jnp.where(low_head, rot, seg))
    return jnp.concatenate(outs, axis=1)


def _inproj_kernel(x_ref, g_ref, w_ref, cos_ref, sin_ref,
                   z_ref, qs_ref, ks_ref, vs_ref, qd_ref, kd_ref, vd_ref):
    x = x_ref[...]
    ms = jnp.mean(x * x, axis=-1, keepdims=True)
    h = (x * lax.rsqrt(ms + RMS_EPS) * g_ref[...]).astype(BF16)
    cos = cos_ref[...]
    sin_signed = sin_ref[...]
    lane = lax.broadcasted_iota(I32, cos.shape, 1)
    first_half = (lane & (HEAD_DIM // 2)) == 0
    nc = 2 * LANES
    q_scale = HEAD_DIM ** -0.5

    def proj(c0):
        return jnp.dot(h, w_ref[:, c0:c0 + nc], preferred_element_type=F32)

    for j in range(CONV_CH // nc):
        a = proj(j * nc)
        gate = proj(CONV_CH + j * nc)
        z_ref[:, j * nc:(j + 1) * nc] = a * jax.nn.sigmoid(gate)
    base = 2 * CONV_CH
    for j in range(SWA_WIDTH // nc):
        r = _rope(proj(base + j * nc), cos, sin_signed, first_half)
        qs_ref[:, j * nc:(j + 1) * nc] = (r * q_scale).astype(BF16)
    base += SWA_WIDTH
    low_head = lane < HEAD_DIM
    for j in range(SWA_KV_WIDTH // nc):
        r = _rope(proj(base + j * nc), cos, sin_signed, first_half)
        ks_ref[:, 2 * j * nc:2 * (j + 1) * nc] = _dup_heads(r, low_head).astype(BF16)
    base += SWA_KV_WIDTH
    for j in range(SWA_KV_WIDTH // nc):
        vs_ref[:, 2 * j * nc:2 * (j + 1) * nc] = _dup_heads(proj(base + j * nc), low_head).astype(BF16)
    base += SWA_KV_WIDTH
    for j in range(DIFF_WIDTH // nc):
        r = _rope(proj(base + j * nc), cos, sin_signed, first_half)
        qd_ref[:, j * nc:(j + 1) * nc] = (r * q_scale).astype(BF16)
    base += DIFF_WIDTH
    for j in range(DIFF_WIDTH // nc):
        r = _rope(proj(base + j * nc), cos, sin_signed, first_half)
        kd_ref[:, j * nc:(j + 1) * nc] = r.astype(BF16)
    base += DIFF_WIDTH
    for j in range(DIFF_WIDTH // nc):
        vd_ref[:, j * nc:(j + 1) * nc] = proj(base + j * nc).astype(BF16)


def _inproj(x2, g, w_bf, cos128, sin128):
    t = x2.shape[0]
    tm = min(INPROJ_ROWS, t)
    row = lambda w: pl.BlockSpec((tm, w), lambda i: (i, 0))
    widths = (CONV_CH, SWA_WIDTH, 2 * SWA_KV_WIDTH, 2 * SWA_KV_WIDTH, DIFF_WIDTH, DIFF_WIDTH, DIFF_WIDTH)
    dtypes = (F32,) + (BF16,) * 6
    return pl.pallas_call(
        _inproj_kernel,
        grid=(t // tm,),
        in_specs=[row(D_MODEL), _resident((1, D_MODEL)), _resident((D_MODEL, IN_WIDTH)),
                  row(LANES), row(LANES)],
        out_specs=[row(w) for w in widths],
        out_shape=[jax.ShapeDtypeStruct((t, w), dt) for w, dt in zip(widths, dtypes)],
        compiler_params=_params("parallel"),
        name="inproj",
    )(x2, g, w_bf, cos128, sin128)


def _conv_kernel(z_ref, halo_ref, dww_ref, dwb_ref, lng_ref, lnb_ref, pw_ref, og_ref,
                 o_ref, buf_ref, y_ref):
    ts = z_ref.shape[1]
    i = pl.program_id(1)
    buf_ref[0:CONV_HALO, :] = jnp.where(i > 0, halo_ref[0], 0.0)
    buf_ref[CONV_HALO:CONV_HALO + ts, :] = z_ref[0]
    rows = 64
    first = CONV_HALO - (CONV_WIDTH - 1)
    for c in range(CONV_CH // LANES):
        cs = slice(c * LANES, (c + 1) * LANES)
        for r in range(ts // rows):
            acc = jnp.broadcast_to(dwb_ref[:, cs], (rows, LANES))
            for j in range(CONV_WIDTH):
                s0 = first + j + r * rows
                acc = acc + buf_ref[s0:s0 + rows, cs] * dww_ref[j:j + 1, cs]
            y_ref[r * rows:(r + 1) * rows, cs] = acc
    y = y_ref[...]
    mu = jnp.mean(y, axis=-1, keepdims=True)
    yc = y - mu
    var = jnp.mean(yc * yc, axis=-1, keepdims=True)
    yn = yc * lax.rsqrt(var + LN_EPS) * lng_ref[...] + lnb_ref[...]
    act = yn * jax.nn.sigmoid(yn)
    p = jnp.dot(act.astype(BF16), pw_ref[...], preferred_element_type=F32)
    ms = jnp.mean(p * p, axis=-1, keepdims=True)
    o_ref[0] = (p * lax.rsqrt(ms + RMS_EPS) * og_ref[...]).astype(BF16)


def _conv(z3, dw_w, dw_b, ln_g, ln_b, pw_bf, out_g):
    b, s, _ = z3.shape
    ts = min(CONV_ROWS, s)
    hb = ts // CONV_HALO
    return pl.pallas_call(
        _conv_kernel,
        grid=(b, s // ts),
        in_specs=[pl.BlockSpec((1, ts, CONV_CH), lambda bi, i: (bi, i, 0)),
                  pl.BlockSpec((1, CONV_HALO, CONV_CH), lambda bi, i: (bi, jnp.maximum(i * hb - 1, 0), 0)),
                  _resident((CONV_WIDTH, CONV_CH)), _resident((1, CONV_CH)), _resident((1, CONV_CH)),
                  _resident((1, CONV_CH)), _resident((CONV_CH, CONV_CH)), _resident((1, CONV_CH))],
        out_specs=pl.BlockSpec((1, ts, CONV_CH), lambda bi, i: (bi, i, 0)),
        out_shape=jax.ShapeDtypeStruct((b, s, CONV_CH), BF16),
        scratch_shapes=[pltpu.VMEM((CONV_HALO + ts, CONV_CH), F32), pltpu.VMEM((ts, CONV_CH), F32)],
        compiler_params=_params("parallel", "arbitrary"),
        name="conv",
    )(z3, z3, dw_w, dw_b, ln_g, ln_b, pw_bf, out_g)


def _swa_kernel(q_ref, kc_ref, kp_ref, vc_ref, vp_ref, sink_ref, g_ref, o_ref):
    w = SWA_WINDOW
    n = pl.program_id(1)
    rows = SWA_GROUP * w
    qi = lax.broadcasted_iota(I32, (rows, 2 * w), 0) & (w - 1)
    kj = lax.broadcasted_iota(I32, (rows, 2 * w), 1)
    valid = ((kj < w) & (kj > qi) & (n > 0)) | ((kj >= w) & (kj - w <= qi))
    low_head = lax.broadcasted_iota(I32, (w, LANES), 1) < HEAD_DIM
    contract = (((1,), (1,)), ((), ()))
    zero = jnp.zeros((w, LANES), BF16)
    heads = [None] * SWA_Q_HEADS
    for hk in range(SWA_KV_HEADS):
        cs = slice(hk * LANES, (hk + 1) * LANES)
        kk = jnp.concatenate([kp_ref[0, :, cs], kc_ref[0, :, cs]], axis=0)
        vv = jnp.concatenate([vp_ref[0, :, cs], vc_ref[0, :, cs]], axis=0)
        q_parts, sink_parts = [], []
        for gq in range(SWA_GROUP):
            hq = hk * SWA_GROUP + gq
            tile = q_ref[0, :, (hq // 2) * LANES:(hq // 2 + 1) * LANES]
            q_parts.append(jnp.where(low_head, tile, zero) if hq % 2 == 0 else jnp.where(low_head, zero, tile))
            sink_parts.append(jnp.broadcast_to(sink_ref[:, hq:hq + 1], (w, LANES)))
        q3 = jnp.concatenate(q_parts, axis=0)
        sink = jnp.concatenate(sink_parts, axis=0)
        s = lax.dot_general(q3, kk, contract, preferred_element_type=F32)
        s = jnp.where(valid, s, NEG_INF)
        m = jnp.maximum(jnp.max(s, axis=-1, keepdims=True), sink)
        e = jnp.exp(s - jnp.concatenate([m, m], axis=1))
        den = jnp.sum(e, axis=-1, keepdims=True) + jnp.exp(sink - m)
        o = jnp.dot(e.astype(BF16), vv, preferred_element_type=F32) * (1.0 / den)
        for gq in range(SWA_GROUP):
            heads[hk * SWA_GROUP + gq] = o[gq * w:(gq + 1) * w]
    tiles = []
    ss = jnp.zeros((w, 1), F32)
    for t in range(SWA_Q_HEADS // 2):
        tile = jnp.where(low_head, heads[2 * t], heads[2 * t + 1])
        ss = ss + jnp.sum(tile * tile, axis=-1, keepdims=True)
        tiles.append(tile)
    scale = lax.rsqrt(ss * (1.0 / SWA_WIDTH) + RMS_EPS)
    o_ref[0] = (jnp.concatenate(tiles, axis=1) * scale * g_ref[...]).astype(BF16)


def _swa(q3, k3, v3, sinks, out_g):
    b, s, _ = q3.shape
    w = SWA_WINDOW
    kvw = k3.shape[2]
    cur = lambda width: pl.BlockSpec((1, w, width), lambda bi, n: (bi, n, 0))
    prev = lambda width: pl.BlockSpec((1, w, width), lambda bi, n: (bi, jnp.maximum(n - 1, 0), 0))
    return pl.pallas_call(
        _swa_kernel,
        grid=(b, s // w),
        in_specs=[cur(SWA_WIDTH), cur(kvw), prev(kvw), cur(kvw), prev(kvw),
                  _resident((1, SWA_Q_HEADS)), _resident((1, SWA_WIDTH))],
        out_specs=cur(SWA_WIDTH),
        out_shape=jax.ShapeDtypeStruct((b, s, SWA_WIDTH), BF16),
        compiler_params=_params("parallel", "arbitrary"),
        name="swa",
    )(q3, k3, k3, v3, v3, sinks, out_g)


def _diff_kernel(q_ref, k_ref, v_ref, lq1_ref, lk1_ref, lq2_ref, lk2_ref, g_ref, o_ref,
                 q2_ref, m_ref, l_ref, acc_ref, *, lambda_init):
    tq = q_ref.shape[1]
    tk = min(DIFF_K_ROWS, k_ref.shape[1])
    i = pl.program_id(2)
    contract = (((1,), (1,)), ((), ()))
    q = q_ref[0]
    low_head = lax.broadcasted_iota(I32, q.shape, 1) < HEAD_DIM
    zero = jnp.zeros(q.shape, q.dtype)
    q2_ref[0:tq, :] = jnp.where(low_head, q, zero)
    q2_ref[tq:2 * tq, :] = jnp.where(low_head, zero, q)
    m_ref[...] = jnp.full(m_ref.shape, NEG_INF, F32)
    l_ref[...] = jnp.zeros(l_ref.shape, F32)
    acc_ref[...] = jnp.zeros(acc_ref.shape, F32)
    reps = tk // LANES

    def step(j, masked):
        k0 = pl.multiple_of(j * tk, tk)
        kblk = k_ref[0, pl.ds(k0, tk), :]
        vblk = v_ref[0, pl.ds(k0, tk), :]
        s = lax.dot_general(q2_ref[...], kblk, contract, preferred_element_type=F32)
        if masked:
            qpos = i * tq + (lax.broadcasted_iota(I32, (2 * tq, tk), 0) & (tq - 1))
            kpos = k0 + lax.broadcasted_iota(I32, (2 * tq, tk), 1)
            s = jnp.where(kpos <= qpos, s, NEG_INF)
        m_prev = m_ref[...]
        m_next = jnp.maximum(m_prev, jnp.max(s, axis=-1, keepdims=True))
        alpha = jnp.exp(m_prev - m_next)
        p = jnp.exp(s - jnp.concatenate([m_next] * reps, axis=1))
        l_ref[...] = alpha * l_ref[...] + jnp.sum(p, axis=-1, keepdims=True)
        acc_ref[...] = alpha * acc_ref[...] + jnp.dot(p.astype(BF16), vblk, preferred_element_type=F32)
        m_ref[...] = m_next

    n_full = (i * tq) // tk
    n_diag = tq // tk

    def body(j, carry):
        step(j, False)
        return carry

    lax.fori_loop(0, n_full, body, 0)
    for jd in range(n_diag):
        step(n_full + jd, True)

    lam = (jnp.exp(jnp.sum(lq1_ref[...] * lk1_ref[...], axis=-1, keepdims=True))
           - jnp.exp(jnp.sum(lq2_ref[...] * lk2_ref[...], axis=-1, keepdims=True)) + lambda_init)
    o = acc_ref[0:tq, :] / l_ref[0:tq, :] - lam * (acc_ref[tq:2 * tq, :] / l_ref[tq:2 * tq, :])
    ms = jnp.mean(o * o, axis=-1, keepdims=True)
    o_ref[0] = (o * lax.rsqrt(ms + LN_EPS) * g_ref[...] * (1.0 - lambda_init)).astype(BF16)


def _diff(q3, k3, v3, lq1, lk1, lq2, lk2, subln_g, lambda_init):
    b, s, _ = q3.shape
    tq = min(DIFF_Q_ROWS, s)
    assert tq & (tq - 1) == 0
    hw = 2 * HEAD_DIM
    qspec = pl.BlockSpec((1, tq, hw), lambda bi, h, i: (bi, i, h))
    kvspec = pl.BlockSpec((1, s, hw), lambda bi, h, i: (bi, 0, h))
    vec = _resident((1, HEAD_DIM))
    return pl.pallas_call(
        functools.partial(_diff_kernel, lambda_init=lambda_init),
        grid=(b, DIFF_HEADS, s // tq),
        in_specs=[qspec, kvspec, kvspec, vec, vec, vec, vec, _resident((1, hw))],
        out_specs=qspec,
        out_shape=jax.ShapeDtypeStruct((b, s, DIFF_WIDTH), BF16),
        scratch_shapes=[pltpu.VMEM((2 * tq, hw), BF16), pltpu.VMEM((2 * tq, hw), F32),
                        pltpu.VMEM((2 * tq, hw), F32), pltpu.VMEM((2 * tq, hw), F32)],
        compiler_params=_params("parallel", "parallel", "arbitrary"),
        name="diff",
    )(q3, k3, v3, lq1, lk1, lq2, lk2, subln_g)


def _bits(v):
    return lax.bitcast_convert_type(v, U32)


def _route(logits):
    lane = lax.broadcasted_iota(I32, logits.shape, 1)
    big = jnp.int32(ROUTE_LANES)
    is_group = lane < N_GROUPS
    gmax = jnp.max(jnp.where(is_group, logits, NEG_INF), axis=-1, keepdims=True)
    g_idx = jnp.min(jnp.where(is_group & (logits == gmax), lane, big), axis=-1, keepdims=True)
    g_w = 1.0 / jnp.sum(jnp.where(is_group, jnp.exp(logits - gmax), 0.0), axis=-1, keepdims=True)
    lo = N_GROUPS + g_idx * EXPERTS_PER_GROUP
    in_sel = (lane >= lo) & (lane < lo + EXPERTS_PER_GROUP)
    e1 = jnp.max(jnp.where(in_sel, logits, NEG_INF), axis=-1, keepdims=True)
    i1 = jnp.min(jnp.where(in_sel & (logits == e1), lane, big), axis=-1, keepdims=True)
    rest = in_sel & (lane != i1)
    e2 = jnp.max(jnp.where(rest, logits, NEG_INF), axis=-1, keepdims=True)
    i2 = jnp.min(jnp.where(rest & (logits == e2), lane, big), axis=-1, keepdims=True)
    t = jnp.exp(e2 - e1)
    w1 = 1.0 / (1.0 + t)
    w2 = t * w1
    out = jnp.where(lane == 0, (i1 - N_GROUPS).astype(F32),
          jnp.where(lane == 1, (i2 - N_GROUPS).astype(F32),
          jnp.where(lane == 2, g_w * w1,
          jnp.where(lane == 3, g_w * w2, 0.0))))
    return out


def _pack_pair(lo_f32, hi_f32):
    lo = _bits(lo_f32.astype(BF16).astype(F32)) >> 16
    hi = _bits(hi_f32.astype(BF16).astype(F32)) & jnp.uint32(0xFFFF0000)
    return lo | hi


def _unpack_pair(words):
    lo = lax.bitcast_convert_type(words << 16, F32)
    hi = lax.bitcast_convert_type(words & jnp.uint32(0xFFFF0000), F32)
    return lo, hi


def _outproj_kernel(x_ref, c_ref, s_ref, d_ref, w_ref, g_ref, wr_ref, br_ref,
                    xo_ref, hp_ref, route_ref, counts_ref, run_ref):
    tm = x_ref.shape[0]

    @pl.when(pl.program_id(0) == 0)
    def _():
        run_ref[...] = jnp.zeros(run_ref.shape, F32)

    nc = 4 * LANES
    c_in, s_in, d_in = c_ref[...], s_ref[...], d_ref[...]
    o1, o2 = CONV_CH, CONV_CH + SWA_WIDTH
    ss = jnp.zeros((tm, 1), F32)
    for j in range(D_MODEL // nc):
        cs = slice(j * nc, (j + 1) * nc)
        y = (jnp.dot(c_in, w_ref[0:o1, cs], preferred_element_type=F32)
             + jnp.dot(s_in, w_ref[o1:o2, cs], preferred_element_type=F32)
             + jnp.dot(d_in, w_ref[o2:D_MODEL, cs], preferred_element_type=F32))
        xn = x_ref[:, cs] + y
        xo_ref[:, cs] = xn
        ss = ss + jnp.sum(xn * xn, axis=-1, keepdims=True)
    scale = lax.rsqrt(ss * (1.0 / D_MODEL) + RMS_EPS)
    half = D_MODEL // 2
    h_lo = xo_ref[:, 0:half] * scale * g_ref[:, 0:half]
    h_hi = xo_ref[:, half:D_MODEL] * scale * g_ref[:, half:D_MODEL]
    words = _pack_pair(h_lo, h_hi)
    for k in range(TOKEN_TILE_ROWS):
        hp_ref[pl.ds(k, tm, stride=TOKEN_TILE_ROWS), :] = words[:, k * LANES:(k + 1) * LANES]
    b_lo, b_hi = h_lo.astype(BF16), h_hi.astype(BF16)
    r_lo = (h_lo - b_lo.astype(F32)).astype(BF16)
    r_hi = (h_hi - b_hi.astype(F32)).astype(BF16)
    acc = (jnp.dot(b_lo, wr_ref[0:half, :], preferred_element_type=F32)
           + jnp.dot(b_hi, wr_ref[half:D_MODEL, :], preferred_element_type=F32)
           + jnp.dot(r_lo, wr_ref[0:half, :], preferred_element_type=F32)
           + jnp.dot(r_hi, wr_ref[half:D_MODEL, :], preferred_element_type=F32))
    logits = acc[:, 0:ROUTE_LANES] + acc[:, ROUTE_LANES:2 * ROUTE_LANES] + br_ref[...]
    route = _route(logits)
    lane = lax.broadcasted_iota(I32, route.shape, 1)
    hot1 = (lane == route[:, 0:1].astype(I32)).astype(F32)
    hot2 = (lane == route[:, 1:2].astype(I32)).astype(F32)
    hot = hot1 + hot2
    tri = (lax.broadcasted_iota(I32, (tm, tm), 1) < lax.broadcasted_iota(I32, (tm, tm), 0)).astype(BF16)
    before = jnp.dot(tri, hot.astype(BF16), preferred_element_type=F32) + run_ref[...]
    rank1 = jnp.sum(hot1 * before, axis=-1, keepdims=True)
    rank2 = jnp.sum(hot2 * before, axis=-1, keepdims=True)
    route_ref[...] = jnp.where(lane == 4, rank1, jnp.where(lane == 5, rank2, route))
    run_ref[...] = run_ref[...] + jnp.sum(hot, axis=0, keepdims=True)
    counts_ref[...] = run_ref[...]


def _outproj(x2, conv_o, swa_o, diff_o, w_bf, ffn_g, wr_bf, br):
    t = x2.shape[0]
    tm = min(OUTPROJ_ROWS, t)
    row = lambda w: pl.BlockSpec((tm, w), lambda i: (i, 0))
    return pl.pallas_call(
        _outproj_kernel,
        grid=(t // tm,),
        in_specs=[row(D_MODEL), row(CONV_CH), row(SWA_WIDTH), row(DIFF_WIDTH),
                  _resident((D_MODEL, D_MODEL)), _resident((1, D_MODEL)),
                  _resident((D_MODEL, 2 * ROUTE_LANES)), _resident((1, ROUTE_LANES))],
        out_specs=[row(D_MODEL), pl.BlockSpec((tm * TOKEN_TILE_ROWS, LANES), lambda i: (i, 0)),
                   row(ROUTE_LANES), pl.BlockSpec((1, ROUTE_LANES), lambda i: (0, 0))],
        out_shape=[jax.ShapeDtypeStruct((t, D_MODEL), F32),
                   jax.ShapeDtypeStruct((t * TOKEN_TILE_ROWS, LANES), U32),
                   jax.ShapeDtypeStruct((t, ROUTE_LANES), F32),
                   jax.ShapeDtypeStruct((1, ROUTE_LANES), F32)],
        scratch_shapes=[pltpu.VMEM((1, ROUTE_LANES), F32)],
        compiler_params=_params("arbitrary"),
        name="outproj",
    )(x2, conv_o, swa_o, diff_o, w_bf, ffn_g, wr_bf, br)


def _dispatch_kernel(slot_ref, hp_ref, init_ref, xs_ref, sem):
    del init_ref
    n = slot_ref.shape[2]
    rows = TOKEN_TILE_ROWS

    def copy(a):
        src = pl.multiple_of((a // 2) * rows, rows)
        dst = pl.multiple_of(slot_ref[0, 0, a] * rows, rows)
        return pltpu.make_async_copy(hp_ref.at[pl.ds(src, rows), :], xs_ref.at[pl.ds(dst, rows), :], sem)

    def start(a, c):
        copy(a).start()
        return c
    lax.fori_loop(0, n, start, 0)

    def wait(a, c):
        copy(a).wait()
        return c
    lax.fori_loop(0, n, wait, 0)


def _dispatch(slot3, hp, xs_init):
    nt, _, n = slot3.shape
    td = n // 2
    return pl.pallas_call(
        _dispatch_kernel,
        grid=(nt,),
        in_specs=[pl.BlockSpec((1, 1, n), lambda i: (i, 0, 0), memory_space=pltpu.SMEM),
                  pl.BlockSpec((td * TOKEN_TILE_ROWS, LANES), lambda i: (i, 0)),
                  pl.BlockSpec(memory_space=pl.ANY)],
        out_specs=pl.BlockSpec(memory_space=pl.ANY),
        out_shape=jax.ShapeDtypeStruct(xs_init.shape, xs_init.dtype),
        scratch_shapes=[pltpu.SemaphoreType.DMA],
        input_output_aliases={2: 0},
        compiler_params=_params("arbitrary"),
        name="dispatch",
    )(slot3, hp, xs_init)


def _experts_kernel(be_ref, bfirst_ref, bslot_ref, bnext_ref, bcnt_ref,
                    xs_ref, w13_hbm, w2_hbm, ys_ref,
                    w13_buf, w2_buf, w13_bf, w2_bf, sem13, sem2, *, layer):
    i = pl.program_id(0)
    e = be_ref[i]
    valid = e >= 0
    slot = bslot_ref[i]

    def w_copies(expert, s):
        return (pltpu.make_async_copy(w13_hbm.at[layer, expert], w13_buf.at[s], sem13.at[s]),
                pltpu.make_async_copy(w2_hbm.at[layer, expert], w2_buf.at[s], sem2.at[s]))

    @pl.when(valid & (bfirst_ref[i] == 1))
    def _():
        @pl.when(i == 0)
        def _():
            for cp in w_copies(e, slot):
                cp.start()
        for cp in w_copies(e, slot):
            cp.wait()
        nxt = bnext_ref[i]

        @pl.when(nxt >= 0)
        def _():
            for cp in w_copies(nxt, 1 - slot):
                cp.start()
        w13_bf[...] = w13_buf[slot].astype(BF16)
        w2_bf[...] = w2_buf[slot].astype(BF16)

    @pl.when(valid)
    def _():
        half = D_MODEL // 2
        tm = xs_ref.shape[0] // TOKEN_TILE_ROWS
        live = lax.broadcasted_iota(I32, (tm, LANES), 0) < bcnt_ref[i]
        los, his = [], []
        for k in range(TOKEN_TILE_ROWS):
            words = jnp.where(live, xs_ref[pl.ds(k, tm, stride=TOKEN_TILE_ROWS), :], jnp.uint32(0))
            lo, hi = _unpack_pair(words)
            los.append(lo.astype(BF16))
            his.append(hi.astype(BF16))
        x_lo = jnp.concatenate(los, axis=1)
        x_hi = jnp.concatenate(his, axis=1)
        h = (jnp.dot(x_lo, w13_bf[0:half, :], preferred_element_type=F32)
             + jnp.dot(x_hi, w13_bf[half:D_MODEL, :], preferred_element_type=F32))
        a, b = h[:, 0:EXPERT_FF], h[:, EXPERT_FF:2 * EXPERT_FF]
        act = (a * jax.nn.sigmoid(a) * b).astype(BF16)
        y = jnp.dot(act, w2_bf[...], preferred_element_type=F32)
        for k in range(TOKEN_TILE_ROWS):
            ys_ref[pl.ds(k, tm, stride=TOKEN_TILE_ROWS), :] = _pack_pair(
                y[:, k * LANES:(k + 1) * LANES], y[:, half + k * LANES:half + (k + 1) * LANES])

    @pl.when(jnp.logical_not(valid))
    def _():
        ys_ref[...] = jnp.zeros(ys_ref.shape, ys_ref.dtype)


def _experts(tables, xs, w13, w2, layer):
    nblk = tables[0].shape[0]
    rows = xs.shape[0] // nblk
    blk = pl.BlockSpec((rows, LANES), lambda i, *_: (i, 0))
    return pl.pallas_call(
        functools.partial(_experts_kernel, layer=layer),
        grid_spec=pltpu.PrefetchScalarGridSpec(
            num_scalar_prefetch=len(tables),
            grid=(nblk,),
            in_specs=[blk, pl.BlockSpec(memory_space=pl.ANY), pl.BlockSpec(memory_space=pl.ANY)],
            out_specs=blk,
            scratch_shapes=[pltpu.VMEM((2, D_MODEL, 2 * EXPERT_FF), F32),
                            pltpu.VMEM((2, EXPERT_FF, D_MODEL), F32),
                            pltpu.VMEM((D_MODEL, 2 * EXPERT_FF), BF16),
                            pltpu.VMEM((EXPERT_FF, D_MODEL), BF16),
                            pltpu.SemaphoreType.DMA((2,)), pltpu.SemaphoreType.DMA((2,))],
        ),
        out_shape=jax.ShapeDtypeStruct(xs.shape, U32),
        compiler_params=_params("arbitrary"),
        name="experts",
    )(*tables, xs, w13, w2)


def _combine_kernel(slot_ref, x_ref, route_ref, ys_hbm, g_ref, o_ref, buf, sem, *, final_norm):
    tm = x_ref.shape[0]
    n = slot_ref.shape[2]
    rows = TOKEN_TILE_ROWS
    half = D_MODEL // 2

    def copy(a):
        src = pl.multiple_of(slot_ref[0, 0, a] * rows, rows)
        dst = pl.multiple_of(a * rows, rows)
        return pltpu.make_async_copy(ys_hbm.at[pl.ds(src, rows), :], buf.at[pl.ds(dst, rows), :], sem)

    def start(a, c):
        copy(a).start()
        return c
    lax.fori_loop(0, n, start, 0)

    def wait(a, c):
        copy(a).wait()
        return c
    lax.fori_loop(0, n, wait, 0)

    g1 = route_ref[:, 2:3]
    g2 = route_ref[:, 3:4]
    ss = jnp.zeros((tm, 1), F32)
    for k in range(rows):
        lo1, hi1 = _unpack_pair(buf[pl.ds(k, tm, stride=rows), :])
        lo2, hi2 = _unpack_pair(buf[pl.ds(tm * rows + k, tm, stride=rows), :])
        c_lo = slice(k * LANES, (k + 1) * LANES)
        c_hi = slice(half + k * LANES, half + (k + 1) * LANES)
        x_lo = x_ref[:, c_lo] + (g1 * lo1 + g2 * lo2)
        x_hi = x_ref[:, c_hi] + (g1 * hi1 + g2 * hi2)
        o_ref[:, c_lo] = x_lo
        o_ref[:, c_hi] = x_hi
        if final_norm:
            ss = (ss + jnp.sum(x_lo * x_lo, axis=-1, keepdims=True)
                  + jnp.sum(x_hi * x_hi, axis=-1, keepdims=True))
    if final_norm:
        scale = lax.rsqrt(ss * (1.0 / D_MODEL) + RMS_EPS)
        o_ref[...] = o_ref[...] * scale * g_ref[...]


def _combine(slot3, x2, route, ys, final_g, final_norm):
    t = x2.shape[0]
    nt, _, n = slot3.shape
    tm = n // 2
    row = lambda w: pl.BlockSpec((tm, w), lambda i: (i, 0))
    return pl.pallas_call(
        functools.partial(_combine_kernel, final_norm=final_norm),
        grid=(nt,),
        in_specs=[pl.BlockSpec((1, 1, n), lambda i: (i, 0, 0), memory_space=pltpu.SMEM),
                  row(D_MODEL), row(ROUTE_LANES), pl.BlockSpec(memory_space=pl.ANY),
                  _resident((1, D_MODEL))],
        out_specs=row(D_MODEL),
        out_shape=jax.ShapeDtypeStruct((t, D_MODEL), F32),
        scratch_shapes=[pltpu.VMEM((n * TOKEN_TILE_ROWS, LANES), U32), pltpu.SemaphoreType.DMA],
        compiler_params=_params("arbitrary"),
        name="combine",
    )(slot3, x2, route, ys, final_g)


def _dispatch_tables(route, counts, t):
    tm = MOE_ROWS
    ids = route[:, 0:2].astype(I32)
    rank = route[:, 4:6].astype(I32)
    counts = counts[0, 0:N_EXPERTS].astype(I32)
    padded = (counts + tm - 1) // tm * tm
    pad_end = jnp.cumsum(padded)
    pad_start = pad_end - padded
    experts = jnp.arange(N_EXPERTS, dtype=I32)
    slot = rank + jnp.sum(jnp.where(ids[..., None] == experts, pad_start, 0), axis=-1)
    nblk = 2 * t // tm + N_EXPERTS
    blk_row = jnp.arange(nblk, dtype=I32) * tm
    blk_valid = blk_row < pad_end[-1]
    be = jnp.minimum(jnp.sum((blk_row[:, None] >= pad_end[None, :]).astype(I32), axis=1), N_EXPERTS - 1)
    onehot = (be[:, None] == experts[None, :]).astype(I32)
    pick = lambda table: jnp.sum(onehot * table[None, :], axis=1)
    blk_e = jnp.where(blk_valid, be, -1)
    blk_off = blk_row - pick(pad_start)
    blk_first = (blk_valid & (blk_off == 0)).astype(I32)
    blk_cnt = jnp.where(blk_valid, jnp.clip(pick(counts) - blk_off, 0, tm), 0)
    used = counts > 0
    order = jnp.cumsum(used.astype(I32)) - 1
    cand = jnp.where(used[None, :] & (experts[None, :] > experts[:, None]), experts[None, :], N_EXPERTS)
    nxt = jnp.min(cand, axis=1)
    nxt = jnp.where(nxt >= N_EXPERTS, -1, nxt)
    tables = (blk_e, blk_first, pick(order) % 2, pick(nxt), blk_cnt)
    return slot, tuple(tb.astype(I32) for tb in tables)


def _moe(x2, hp, route, counts, w13, w2, layer, final_g, final_norm):
    t = x2.shape[0]
    slot, tables = _dispatch_tables(route, counts, t)
    td = min(DISPATCH_ROWS, t)
    nblk = tables[0].shape[0]
    xs_init = jnp.zeros((nblk * MOE_ROWS * TOKEN_TILE_ROWS, LANES), U32)
    xs = _dispatch(slot.reshape(t // td, 1, 2 * td), hp, xs_init)
    ys = _experts(tables, xs, w13, w2, layer)
    tc = min(COMBINE_ROWS, t)
    slot3 = slot.reshape(t // tc, tc, 2).transpose(0, 2, 1).reshape(t // tc, 1, 2 * tc)
    return _combine(slot3, x2, route, ys, final_g, final_norm)


def _rope_tables(positions):
    inv_freq = ROPE_THETA ** (-jnp.arange(0, HEAD_DIM, 2, dtype=F32) / HEAD_DIM)
    ang = positions.astype(F32)[..., None] * inv_freq
    c, s = jnp.cos(ang), jnp.sin(ang)
    t = positions.size
    cos128 = jnp.concatenate([c, c, c, c], axis=-1).reshape(t, LANES)
    sin128 = jnp.concatenate([-s, s, -s, s], axis=-1).reshape(t, LANES)
    return cos128, sin128


def _router_weights(wg, we, bg, be):
    w = jnp.zeros((D_MODEL, ROUTE_LANES), F32).at[:, 0:N_GROUPS].set(wg)
    w = w.at[:, N_GROUPS:N_GROUPS + N_EXPERTS].set(we)
    head = w.astype(BF16)
    tail = (w - head.astype(F32)).astype(BF16)
    bias = jnp.zeros((1, ROUTE_LANES), F32).at[0, 0:N_GROUPS].set(bg)
    bias = bias.at[0, N_GROUPS:N_GROUPS + N_EXPERTS].set(be)
    return jnp.concatenate([head, tail], axis=1), bias


def kernel(x, positions, attn_norm_g, w_in, conv_dw_w, conv_dw_b, conv_ln_g, conv_ln_b, conv_pw_w, conv_out_g, swa_sinks, swa_out_g, diff_lambda_q1, diff_lambda_k1, diff_lambda_q2, diff_lambda_k2, diff_subln_g, w_out, ffn_norm_g, router_group_w, router_group_b, router_expert_w, router_expert_b, moe_w13, moe_w2, final_norm_g):
    b, s, d = x.shape
    t = b * s
    depth = w_in.shape[0]
    cos128, sin128 = _rope_tables(positions)
    x2 = x.reshape(t, d)
    row = lambda v: v.reshape(1, -1)
    for i in range(depth):
        lambda_init = 0.8 - 0.6 * float(np.exp(-0.3 * i))
        z, qs, ks, vs, qd, kd, vd = _inproj(x2, row(attn_norm_g[i]), w_in[i].astype(BF16), cos128, sin128)
        conv_o = _conv(z.reshape(b, s, CONV_CH), conv_dw_w[i], row(conv_dw_b[i]), row(conv_ln_g[i]),
                       row(conv_ln_b[i]), conv_pw_w[i].astype(BF16), row(conv_out_g[i]))
        swa_o = _swa(qs.reshape(b, s, -1), ks.reshape(b, s, -1), vs.reshape(b, s, -1),
                     row(swa_sinks[i]), row(swa_out_g[i]))
        diff_o = _diff(qd.reshape(b, s, -1), kd.reshape(b, s, -1), vd.reshape(b, s, -1),
                       row(diff_lambda_q1[i]), row(diff_lambda_k1[i]), row(diff_lambda_q2[i]),
                       row(diff_lambda_k2[i]), row(diff_subln_g[i]), lambda_init)
        wr, br = _router_weights(router_group_w[i], router_expert_w[i], router_group_b[i], router_expert_b[i])
        x2, hp, route, counts = _outproj(x2, conv_o.reshape(t, -1), swa_o.reshape(t, -1),
                                         diff_o.reshape(t, -1), w_out[i].astype(BF16), row(ffn_norm_g[i]),
                                         wr, br)
        x2 = _moe(x2, hp, route, counts, moe_w13, moe_w2, i, row(final_norm_g), i == depth - 1)
    return x2.reshape(b, s, d)
```

```python
import functools

import numpy as np
import jax
import jax.numpy as jnp
from jax import lax
from jax.experimental import pallas as pl
from jax.experimental.pallas import tpu as pltpu

F32 = jnp.float32
BF16 = jnp.bfloat16
I32 = jnp.int32
U32 = jnp.uint32

D_MODEL = 2048
HEAD_DIM = 64
CONV_CH = 512
CONV_WIDTH = 31
SWA_Q_HEADS = 12
SWA_KV_HEADS = 4
SWA_GROUP = SWA_Q_HEADS // SWA_KV_HEADS
SWA_WINDOW = 128
SWA_WIDTH = SWA_Q_HEADS * HEAD_DIM
SWA_KV_WIDTH = SWA_KV_HEADS * HEAD_DIM
DIFF_HEADS = 6
DIFF_WIDTH = DIFF_HEADS * 2 * HEAD_DIM
IN_WIDTH = 2 * CONV_CH + SWA_WIDTH + 2 * SWA_KV_WIDTH + 3 * DIFF_WIDTH
ROPE_THETA = 10000.0
N_GROUPS = 8
EXPERTS_PER_GROUP = 8
N_EXPERTS = N_GROUPS * EXPERTS_PER_GROUP
EXPERT_FF = 512
RMS_EPS = 1e-6
LN_EPS = 1e-5
NEG_INF = -1e30
LOG2_E = 1.4426950408889634

LANES = 128
SUBLANES = 8
VMEM_LIMIT_BYTES = 56 * 1024 * 1024

INPROJ_ROWS = 512
CONV_ROWS = 256
CONV_HALO = 32
DIFF_ROWS = 512
OUTPROJ_ROWS = 512
OUTPROJ_SPLIT = 1
MOE_ROWS = 128
DISPATCH_ROWS = 256
COMBINE_ROWS = 256
TOKEN_TILE_ROWS = D_MODEL // 2 // LANES
DMA_UNROLL = 8
W13_CHUNKS = 4
W2_CHUNKS = 2
ROUTE_LANES = 128


def _params(*sem):
    return pltpu.CompilerParams(dimension_semantics=sem, vmem_limit_bytes=VMEM_LIMIT_BYTES)


def _resident(shape):
    nd = len(shape)
    return pl.BlockSpec(shape, lambda *_: (0,) * nd, pipeline_mode=pl.Buffered(1))


def _rope(r, cos, sin_signed, first_half):
    outs = []
    for k in range(r.shape[1] // LANES):
        seg = r[:, k * LANES:(k + 1) * LANES]
        partner = jnp.where(first_half,
                            pltpu.roll(seg, LANES - HEAD_DIM // 2, 1),
                            pltpu.roll(seg, HEAD_DIM // 2, 1))
        outs.append(seg * cos + partner * sin_signed)
    return outs[0] if len(outs) == 1 else jnp.concatenate(outs, axis=1)


def _dup_heads(r, low_head):
    outs = []
    for k in range(r.shape[1] // LANES):
        seg = r[:, k * LANES:(k + 1) * LANES]
        rot = pltpu.roll(seg, HEAD_DIM, 1)
        outs.append(jnp.where(low_head, seg, rot))
        outs.append(jnp.where(low_head, rot, seg))
    return jnp.concatenate(outs, axis=1)


def _inproj_kernel(x_ref, g_ref, w_ref, cos_ref, sin_ref,
                   z_ref, qs_ref, ks_ref, vs_ref, qd_ref, kd_ref, vd_ref):
    x = x_ref[...]
    ms = jnp.mean(x * x, axis=-1, keepdims=True)
    h = (x * lax.rsqrt(ms + RMS_EPS) * g_ref[...]).astype(BF16)
    cos = cos_ref[...]
    sin_signed = sin_ref[...]
    lane = lax.broadcasted_iota(I32, cos.shape, 1)
    first_half = (lane & (HEAD_DIM // 2)) == 0
    nc = 2 * LANES
    q_scale = HEAD_DIM ** -0.5

    def proj(c0):
        return jnp.dot(h, w_ref[:, c0:c0 + nc], preferred_element_type=F32)

    for j in range(CONV_CH // nc):
        a = proj(j * nc)
        gate = proj(CONV_CH + j * nc)
        z_ref[:, j * nc:(j + 1) * nc] = a * jax.nn.sigmoid(gate)
    base = 2 * CONV_CH
    for j in range(SWA_WIDTH // nc):
        r = _rope(proj(base + j * nc), cos, sin_signed, first_half)
        qs_ref[:, j * nc:(j + 1) * nc] = (r * q_scale).astype(BF16)
    base += SWA_WIDTH
    low_head = lane < HEAD_DIM
    for j in range(SWA_KV_WIDTH // nc):
        r = _rope(proj(base + j * nc), cos, sin_signed, first_half)
        ks_ref[:, 2 * j * nc:2 * (j + 1) * nc] = _dup_heads(r, low_head).astype(BF16)
    base += SWA_KV_WIDTH
    for j in range(SWA_KV_WIDTH // nc):
        vs_ref[:, 2 * j * nc:2 * (j + 1) * nc] = _dup_heads(proj(base + j * nc), low_head).astype(BF16)
    base += SWA_KV_WIDTH
    for j in range(DIFF_WIDTH // nc):
        r = _rope(proj(base + j * nc), cos, sin_signed, first_half)
        qd_ref[:, j * nc:(j + 1) * nc] = (r * (q_scale * LOG2_E)).astype(BF16)
    base += DIFF_WIDTH
    for j in range(DIFF_WIDTH // nc):
        r = _rope(proj(base + j * nc), cos, sin_signed, first_half)
        kd_ref[:, j * nc:(j + 1) * nc] = r.astype(BF16)
    base += DIFF_WIDTH
    for j in range(DIFF_WIDTH // nc):
        vd_ref[:, j * nc:(j + 1) * nc] = proj(base + j * nc).astype(BF16)


def _inproj(x2, g, w_bf, cos128, sin128):
    t = x2.shape[0]
    tm = min(INPROJ_ROWS, t)
    row = lambda w: pl.BlockSpec((tm, w), lambda i: (i, 0))
    widths = (CONV_CH, SWA_WIDTH, 2 * SWA_KV_WIDTH, 2 * SWA_KV_WIDTH, DIFF_WIDTH, DIFF_WIDTH, DIFF_WIDTH)
    dtypes = (F32,) + (BF16,) * 6
    return pl.pallas_call(
        _inproj_kernel,
        grid=(t // tm,),
        in_specs=[row(D_MODEL), _resident((1, D_MODEL)), _resident((D_MODEL, IN_WIDTH)),
                  row(LANES), row(LANES)],
        out_specs=[row(w) for w in widths],
        out_shape=[jax.ShapeDtypeStruct((t, w), dt) for w, dt in zip(widths, dtypes)],
        compiler_params=_params("parallel"),
        name="inproj",
    )(x2, g, w_bf, cos128, sin128)


def _conv_kernel(z_ref, halo_ref, dww_ref, dwb_ref, lng_ref, lnb_ref, pw_ref, og_ref,
                 o_ref, buf_ref, y_ref):
    ts = z_ref.shape[1]
    i = pl.program_id(1)
    buf_ref[0, 0:CONV_HALO, :] = jnp.where(i > 0, halo_ref[0], 0.0)
    buf_ref[0, CONV_HALO:CONV_HALO + ts, :] = z_ref[0]
    kept = CONV_HALO + ts - SUBLANES
    for b in range(1, SUBLANES):
        buf_ref[b, 0:kept, :] = buf_ref[0, b:b + kept, :]
    rows = 64
    first = CONV_HALO - (CONV_WIDTH - 1)
    for c in range(CONV_CH // LANES):
        cs = slice(c * LANES, (c + 1) * LANES)
        for r in range(ts // rows):
            acc = jnp.broadcast_to(dwb_ref[:, cs], (rows, LANES))
            for j in range(CONV_WIDTH):
                b = (first + j) % SUBLANES
                s0 = first + j - b + r * rows
                acc = acc + buf_ref[b, s0:s0 + rows, cs] * dww_ref[j:j + 1, cs]
            y_ref[r * rows:(r + 1) * rows, cs] = acc
    y = y_ref[...]
    mu = jnp.mean(y, axis=-1, keepdims=True)
    yc = y - mu
    var = jnp.mean(yc * yc, axis=-1, keepdims=True)
    yn = yc * lax.rsqrt(var + LN_EPS) * lng_ref[...] + lnb_ref[...]
    act = yn * jax.nn.sigmoid(yn)
    p = jnp.dot(act.astype(BF16), pw_ref[...], preferred_element_type=F32)
    ms = jnp.mean(p * p, axis=-1, keepdims=True)
    o_ref[0] = (p * lax.rsqrt(ms + RMS_EPS) * og_ref[...]).astype(BF16)


def _conv(z3, dw_w, dw_b, ln_g, ln_b, pw_bf, out_g):
    b, s, _ = z3.shape
    ts = min(CONV_ROWS, s)
    hb = ts // CONV_HALO
    return pl.pallas_call(
        _conv_kernel,
        grid=(b, s // ts),
        in_specs=[pl.BlockSpec((1, ts, CONV_CH), lambda bi, i: (bi, i, 0)),
                  pl.BlockSpec((1, CONV_HALO, CONV_CH), lambda bi, i: (bi, jnp.maximum(i * hb - 1, 0), 0)),
                  _resident((CONV_WIDTH, CONV_CH)), _resident((1, CONV_CH)), _resident((1, CONV_CH)),
                  _resident((1, CONV_CH)), _resident((CONV_CH, CONV_CH)), _resident((1, CONV_CH))],
        out_specs=pl.BlockSpec((1, ts, CONV_CH), lambda bi, i: (bi, i, 0)),
        out_shape=jax.ShapeDtypeStruct((b, s, CONV_CH), BF16),
        scratch_shapes=[pltpu.VMEM((SUBLANES, CONV_HALO + ts, CONV_CH), F32), pltpu.VMEM((ts, CONV_CH), F32)],
        compiler_params=_params("parallel", "arbitrary"),
        name="conv",
    )(z3, z3, dw_w, dw_b, ln_g, ln_b, pw_bf, out_g)


def _swa_kernel(q_ref, kc_ref, kp_ref, vc_ref, vp_ref, sink_ref, g_ref, o_ref):
    w = SWA_WINDOW
    n = pl.program_id(1)
    rows = SWA_GROUP * w
    qi = lax.broadcasted_iota(I32, (rows, 2 * w), 0) & (w - 1)
    kj = lax.broadcasted_iota(I32, (rows, 2 * w), 1)
    valid = ((kj < w) & (kj > qi) & (n > 0)) | ((kj >= w) & (kj - w <= qi))
    low_head = lax.broadcasted_iota(I32, (w, LANES), 1) < HEAD_DIM
    contract = (((1,), (1,)), ((), ()))
    zero = jnp.zeros((w, LANES), BF16)
    heads = [None] * SWA_Q_HEADS
    for hk in range(SWA_KV_HEADS):
        cs = slice(hk * LANES, (hk + 1) * LANES)
        kk = jnp.concatenate([kp_ref[0, :, cs], kc_ref[0, :, cs]], axis=0)
        vv = jnp.concatenate([vp_ref[0, :, cs], vc_ref[0, :, cs]], axis=0)
        q_parts, sink_parts = [], []
        for gq in range(SWA_GROUP):
            hq = hk * SWA_GROUP + gq
            tile = q_ref[0, :, (hq // 2) * LANES:(hq // 2 + 1) * LANES]
            q_parts.append(jnp.where(low_head, tile, zero) if hq % 2 == 0 else jnp.where(low_head, zero, tile))
            sink_parts.append(jnp.broadcast_to(sink_ref[:, hq:hq + 1], (w, LANES)))
        q3 = jnp.concatenate(q_parts, axis=0)
        sink = jnp.concatenate(sink_parts, axis=0)
        s = lax.dot_general(q3, kk, contract, preferred_element_type=F32)
        s = jnp.where(valid, s, NEG_INF)
        m = jnp.maximum(jnp.max(s, axis=-1, keepdims=True), sink)
        e = jnp.exp(s - jnp.concatenate([m, m], axis=1))
        den = jnp.sum(e, axis=-1, keepdims=True) + jnp.exp(sink - m)
        o = jnp.dot(e.astype(BF16), vv, preferred_element_type=F32) * (1.0 / den)
        for gq in range(SWA_GROUP):
            heads[hk * SWA_GROUP + gq] = o[gq * w:(gq + 1) * w]
    tiles = []
    ss = jnp.zeros((w, 1), F32)
    for t in range(SWA_Q_HEADS // 2):
        tile = jnp.where(low_head, heads[2 * t], heads[2 * t + 1])
        ss = ss + jnp.sum(tile * tile, axis=-1, keepdims=True)
        tiles.append(tile)
    scale = lax.rsqrt(ss * (1.0 / SWA_WIDTH) + RMS_EPS)
    o_ref[0] = (jnp.concatenate(tiles, axis=1) * scale * g_ref[...]).astype(BF16)


def _swa(q3, k3, v3, sinks, out_g):
    b, s, _ = q3.shape
    w = SWA_WINDOW
    kvw = k3.shape[2]
    cur = lambda width: pl.BlockSpec((1, w, width), lambda bi, n: (bi, n, 0))
    prev = lambda width: pl.BlockSpec((1, w, width), lambda bi, n: (bi, jnp.maximum(n - 1, 0), 0))
    return pl.pallas_call(
        _swa_kernel,
        grid=(b, s // w),
        in_specs=[cur(SWA_WIDTH), cur(kvw), prev(kvw), cur(kvw), prev(kvw),
                  _resident((1, SWA_Q_HEADS)), _resident((1, SWA_WIDTH))],
        out_specs=cur(SWA_WIDTH),
        out_shape=jax.ShapeDtypeStruct((b, s, SWA_WIDTH), BF16),
        compiler_params=_params("parallel", "arbitrary"),
        name="swa",
    )(q3, k3, k3, v3, v3, sinks, out_g)


def _diff_kernel(q_ref, k_ref, v_ref, bias_ref, lq1_ref, lk1_ref, lq2_ref, lk2_ref, g_ref, o_ref,
                 q2_ref, m_ref, l_ref, acc_ref, *, lambda_init):
    tq = q_ref.shape[1]
    tk = tq
    i = pl.program_id(2)
    contract = (((1,), (1,)), ((), ()))
    q = q_ref[0]
    low_head = lax.broadcasted_iota(I32, q.shape, 1) < HEAD_DIM
    zero = jnp.zeros(q.shape, q.dtype)
    q2_ref[0:tq, :] = jnp.where(low_head, q, zero)
    q2_ref[tq:2 * tq, :] = jnp.where(low_head, zero, q)
    m_ref[...] = jnp.full(m_ref.shape, NEG_INF, F32)
    l_ref[...] = jnp.zeros(l_ref.shape, F32)
    acc_ref[...] = jnp.zeros(acc_ref.shape, F32)
    reps = tk // LANES

    def step(j, masked):
        k0 = pl.multiple_of(j * tk, tk)
        kblk = k_ref[0, pl.ds(k0, tk), :]
        vblk = v_ref[0, pl.ds(k0, tk), :]
        s = lax.dot_general(q2_ref[...], kblk, contract, preferred_element_type=F32)
        if masked:
            s = s + bias_ref[...]
        m_prev = m_ref[...]
        m_next = jnp.maximum(m_prev, jnp.max(s, axis=-1, keepdims=True))
        alpha = jnp.exp2(m_prev - m_next)
        p = jnp.exp2(s - jnp.concatenate([m_next] * reps, axis=1))
        l_ref[...] = alpha * l_ref[...] + jnp.sum(p, axis=-1, keepdims=True)
        acc_ref[...] = alpha * acc_ref[...] + jnp.dot(p.astype(BF16), vblk, preferred_element_type=F32)
        m_ref[...] = m_next

    def body(j, carry):
        step(j, False)
        return carry

    lax.fori_loop(0, i, body, 0)
    step(i, True)

    lam = (jnp.exp(jnp.sum(lq1_ref[...] * lk1_ref[...], axis=-1, keepdims=True))
           - jnp.exp(jnp.sum(lq2_ref[...] * lk2_ref[...], axis=-1, keepdims=True)) + lambda_init)
    o = acc_ref[0:tq, :] / l_ref[0:tq, :] - lam * (acc_ref[tq:2 * tq, :] / l_ref[tq:2 * tq, :])
    ms = jnp.mean(o * o, axis=-1, keepdims=True)
    o_ref[0] = (o * lax.rsqrt(ms + LN_EPS) * g_ref[...] * (1.0 - lambda_init)).astype(BF16)


def _diff(q3, k3, v3, lq1, lk1, lq2, lk2, subln_g, lambda_init):
    b, s, _ = q3.shape
    tq = min(DIFF_ROWS, s)
    hw = 2 * HEAD_DIM
    qspec = pl.BlockSpec((1, tq, hw), lambda bi, h, i: (bi, i, h))
    kvspec = pl.BlockSpec((1, s, hw), lambda bi, h, i: (bi, 0, h))
    vec = _resident((1, HEAD_DIM))
    qrow = lax.broadcasted_iota(I32, (2, tq, tq), 1).reshape(2 * tq, tq)
    kcol = lax.broadcasted_iota(I32, (2 * tq, tq), 1)
    bias = jnp.where(kcol <= qrow, 0.0, NEG_INF).astype(F32)
    return pl.pallas_call(
        functools.partial(_diff_kernel, lambda_init=lambda_init),
        grid=(b, DIFF_HEADS, s // tq),
        in_specs=[qspec, kvspec, kvspec, _resident((2 * tq, tq)), vec, vec, vec, vec, _resident((1, hw))],
        out_specs=qspec,
        out_shape=jax.ShapeDtypeStruct((b, s, DIFF_WIDTH), BF16),
        scratch_shapes=[pltpu.VMEM((2 * tq, hw), BF16), pltpu.VMEM((2 * tq, hw), F32),
                        pltpu.VMEM((2 * tq, hw), F32), pltpu.VMEM((2 * tq, hw), F32)],
        compiler_params=_params("parallel", "parallel", "arbitrary"),
        name="diff",
    )(q3, k3, v3, bias, lq1, lk1, lq2, lk2, subln_g)


def _bits(v):
    return lax.bitcast_convert_type(v, U32)


def _route(logits):
    lane = lax.broadcasted_iota(I32, logits.shape, 1)
    big = jnp.int32(ROUTE_LANES)
    is_group = lane < N_GROUPS
    gmax = jnp.max(jnp.where(is_group, logits, NEG_INF), axis=-1, keepdims=True)
    g_idx = jnp.min(jnp.where(is_group & (logits == gmax), lane, big), axis=-1, keepdims=True)
    g_w = 1.0 / jnp.sum(jnp.where(is_group, jnp.exp(logits - gmax), 0.0), axis=-1, keepdims=True)
    lo = N_GROUPS + g_idx * EXPERTS_PER_GROUP
    in_sel = (lane >= lo) & (lane < lo + EXPERTS_PER_GROUP)
    e1 = jnp.max(jnp.where(in_sel, logits, NEG_INF), axis=-1, keepdims=True)
    i1 = jnp.min(jnp.where(in_sel & (logits == e1), lane, big), axis=-1, keepdims=True)
    rest = in_sel & (lane != i1)
    e2 = jnp.max(jnp.where(rest, logits, NEG_INF), axis=-1, keepdims=True)
    i2 = jnp.min(jnp.where(rest & (logits == e2), lane, big), axis=-1, keepdims=True)
    t = jnp.exp(e2 - e1)
    w1 = 1.0 / (1.0 + t)
    w2 = t * w1
    out = jnp.where(lane == 0, (i1 - N_GROUPS).astype(F32),
          jnp.where(lane == 1, (i2 - N_GROUPS).astype(F32),
          jnp.where(lane == 2, g_w * w1,
          jnp.where(lane == 3, g_w * w2, 0.0))))
    return out


def _pack_pair(lo_f32, hi_f32):
    lo = _bits(lo_f32.astype(BF16).astype(F32)) >> 16
    hi = _bits(hi_f32.astype(BF16).astype(F32)) & jnp.uint32(0xFFFF0000)
    return lo | hi


def _unpack_pair(words):
    lo = lax.bitcast_convert_type(words << 16, F32)
    hi = lax.bitcast_convert_type(words & jnp.uint32(0xFFFF0000), F32)
    return lo, hi


def _outproj_kernel(x_ref, c_ref, s_ref, d_ref, w_ref, g_ref, wr_ref, br_ref,
                    xo_ref, hp_ref, route_ref, counts_ref, run_ref):
    tm = x_ref.shape[0]

    @pl.when(pl.program_id(0) == 0)
    def _():
        run_ref[...] = jnp.zeros(run_ref.shape, F32)

    nc = 4 * LANES
    o1, o2 = CONV_CH, CONV_CH + SWA_WIDTH
    half = D_MODEL // 2
    sub = tm // OUTPROJ_SPLIT
    tri = (lax.broadcasted_iota(I32, (sub, sub), 1) < lax.broadcasted_iota(I32, (sub, sub), 0)).astype(BF16)
    for r0 in range(0, tm, sub):
        rs = slice(r0, r0 + sub)
        c_in, s_in, d_in = c_ref[rs, :], s_ref[rs, :], d_ref[rs, :]
        ss = jnp.zeros((sub, 1), F32)
        for j in range(D_MODEL // nc):
            cs = slice(j * nc, (j + 1) * nc)
            y = (jnp.dot(c_in, w_ref[0:o1, cs], preferred_element_type=F32)
                 + jnp.dot(s_in, w_ref[o1:o2, cs], preferred_element_type=F32)
                 + jnp.dot(d_in, w_ref[o2:D_MODEL, cs], preferred_element_type=F32))
            xn = x_ref[rs, cs] + y
            xo_ref[rs, cs] = xn
            ss = ss + jnp.sum(xn * xn, axis=-1, keepdims=True)
        scale = lax.rsqrt(ss * (1.0 / D_MODEL) + RMS_EPS)
        h_lo = xo_ref[rs, 0:half] * scale * g_ref[:, 0:half]
        h_hi = xo_ref[rs, half:D_MODEL] * scale * g_ref[:, half:D_MODEL]
        words = _pack_pair(h_lo, h_hi)
        for k in range(TOKEN_TILE_ROWS):
            hp_ref[pl.ds(r0 * TOKEN_TILE_ROWS + k, sub, stride=TOKEN_TILE_ROWS), :] = (
                words[:, k * LANES:(k + 1) * LANES])
        b_lo, b_hi = h_lo.astype(BF16), h_hi.astype(BF16)
        r_lo = (h_lo - b_lo.astype(F32)).astype(BF16)
        r_hi = (h_hi - b_hi.astype(F32)).astype(BF16)
        acc = (jnp.dot(b_lo, wr_ref[0:half, :], preferred_element_type=F32)
               + jnp.dot(b_hi, wr_ref[half:D_MODEL, :], preferred_element_type=F32)
               + jnp.dot(r_lo, wr_ref[0:half, :], preferred_element_type=F32)
               + jnp.dot(r_hi, wr_ref[half:D_MODEL, :], preferred_element_type=F32))
        logits = acc[:, 0:ROUTE_LANES] + acc[:, ROUTE_LANES:2 * ROUTE_LANES] + br_ref[...]
        route = _route(logits)
        lane = lax.broadcasted_iota(I32, route.shape, 1)
        hot1 = (lane == route[:, 0:1].astype(I32)).astype(F32)
        hot2 = (lane == route[:, 1:2].astype(I32)).astype(F32)
        hot = hot1 + hot2
        before = jnp.dot(tri, hot.astype(BF16), preferred_element_type=F32) + run_ref[...]
        rank1 = jnp.sum(hot1 * before, axis=-1, keepdims=True)
        rank2 = jnp.sum(hot2 * before, axis=-1, keepdims=True)
        route_ref[rs, :] = jnp.where(lane == 4, rank1, jnp.where(lane == 5, rank2, route))
        run_ref[...] = run_ref[...] + jnp.sum(hot, axis=0, keepdims=True)
    counts_ref[...] = run_ref[...]


def _outproj(x2, conv_o, swa_o, diff_o, w_bf, ffn_g, wr_bf, br):
    t = x2.shape[0]
    tm = min(OUTPROJ_ROWS, t)
    row = lambda w: pl.BlockSpec((tm, w), lambda i: (i, 0))
    return pl.pallas_call(
        _outproj_kernel,
        grid=(t // tm,),
        in_specs=[row(D_MODEL), row(CONV_CH), row(SWA_WIDTH), row(DIFF_WIDTH),
                  _resident((D_MODEL, D_MODEL)), _resident((1, D_MODEL)),
                  _resident((D_MODEL, 2 * ROUTE_LANES)), _resident((1, ROUTE_LANES))],
        out_specs=[row(D_MODEL), pl.BlockSpec((tm * TOKEN_TILE_ROWS, LANES), lambda i: (i, 0)),
                   row(ROUTE_LANES), pl.BlockSpec((1, ROUTE_LANES), lambda i: (0, 0))],
        out_shape=[jax.ShapeDtypeStruct((t, D_MODEL), F32),
                   jax.ShapeDtypeStruct((t * TOKEN_TILE_ROWS, LANES), U32),
                   jax.ShapeDtypeStruct((t, ROUTE_LANES), F32),
                   jax.ShapeDtypeStruct((1, ROUTE_LANES), F32)],
        scratch_shapes=[pltpu.VMEM((1, ROUTE_LANES), F32)],
        compiler_params=_params("arbitrary"),
        name="outproj",
    )(x2, conv_o, swa_o, diff_o, w_bf, ffn_g, wr_bf, br)


def _dispatch_kernel(slot_ref, hp_ref, init_ref, xs_ref, sem):
    del init_ref
    n = slot_ref.shape[2]
    rows = TOKEN_TILE_ROWS

    def copy(src_row, dst_row):
        return pltpu.make_async_copy(hp_ref.at[pl.ds(src_row, rows), :], xs_ref.at[pl.ds(dst_row, rows), :], sem)

    def start(t, c):
        src = pl.multiple_of(t * rows, rows)
        for k in range(2):
            dst = pl.multiple_of(slot_ref[0, 0, 2 * t + k] * rows, rows)
            copy(src, dst).start(priority=k)
        return c
    lax.fori_loop(0, n // 2, start, 0, unroll=DMA_UNROLL)

    def wait(a, c):
        copy(0, 0).wait()
        return c
    lax.fori_loop(0, n, wait, 0, unroll=2 * DMA_UNROLL)


def _dispatch(slot3, hp, xs_init):
    nt, _, n = slot3.shape
    td = n // 2
    return pl.pallas_call(
        _dispatch_kernel,
        grid=(nt,),
        in_specs=[pl.BlockSpec((1, 1, n), lambda i: (i, 0, 0), memory_space=pltpu.SMEM),
                  pl.BlockSpec((td * TOKEN_TILE_ROWS, LANES), lambda i: (i, 0)),
                  pl.BlockSpec(memory_space=pl.ANY)],
        out_specs=pl.BlockSpec(memory_space=pl.ANY),
        out_shape=jax.ShapeDtypeStruct(xs_init.shape, xs_init.dtype),
        scratch_shapes=[pltpu.SemaphoreType.DMA],
        input_output_aliases={2: 0},
        compiler_params=_params("arbitrary"),
        name="dispatch",
    )(slot3, hp, xs_init)


def _experts_kernel(be_ref, bfirst_ref, bslot_ref, bnext_ref, bcnt_ref,
                    xs_ref, w13_hbm, w2_hbm, ys_ref,
                    w13_buf, w2_buf, w13_bf, w2_bf, sem13, sem2, *, layer):
    i = pl.program_id(0)
    e = be_ref[i]
    valid = e >= 0
    slot = bslot_ref[i]

    def w_copies(expert, s):
        copies = []
        r13 = D_MODEL // W13_CHUNKS
        for c in range(W13_CHUNKS):
            rs = pl.ds(c * r13, r13)
            copies.append(pltpu.make_async_copy(w13_hbm.at[layer, expert, rs, :], w13_buf.at[s, rs, :],
                                                sem13.at[s]))
        r2 = EXPERT_FF // W2_CHUNKS
        for c in range(W2_CHUNKS):
            rs = pl.ds(c * r2, r2)
            copies.append(pltpu.make_async_copy(w2_hbm.at[layer, expert, rs, :], w2_buf.at[s, rs, :],
                                                sem2.at[s]))
        return copies

    def start_all(copies):
        for c, cp in enumerate(copies):
            cp.start(priority=c % 2)

    @pl.when(valid & (bfirst_ref[i] == 1))
    def _():
        @pl.when(i == 0)
        def _():
            start_all(w_copies(e, slot))
        for cp in w_copies(e, slot):
            cp.wait()
        nxt = bnext_ref[i]

        @pl.when(nxt >= 0)
        def _():
            start_all(w_copies(nxt, 1 - slot))
        w13_bf[...] = w13_buf[slot].astype(BF16)
        w2_bf[...] = w2_buf[slot].astype(BF16)

    @pl.when(valid)
    def _():
        half = D_MODEL // 2
        tm = xs_ref.shape[0] // TOKEN_TILE_ROWS
        live = lax.broadcasted_iota(I32, (tm, LANES), 0) < bcnt_ref[i]
        los, his = [], []
        for k in range(TOKEN_TILE_ROWS):
            words = jnp.where(live, xs_ref[pl.ds(k, tm, stride=TOKEN_TILE_ROWS), :], jnp.uint32(0))
            lo, hi = _unpack_pair(words)
            los.append(lo.astype(BF16))
            his.append(hi.astype(BF16))
        x_lo = jnp.concatenate(los, axis=1)
        x_hi = jnp.concatenate(his, axis=1)
        h = (jnp.dot(x_lo, w13_bf[0:half, :], preferred_element_type=F32)
             + jnp.dot(x_hi, w13_bf[half:D_MODEL, :], preferred_element_type=F32))
        a, b = h[:, 0:EXPERT_FF], h[:, EXPERT_FF:2 * EXPERT_FF]
        act = (a * jax.nn.sigmoid(a) * b).astype(BF16)
        y = jnp.dot(act, w2_bf[...], preferred_element_type=F32)
        for k in range(TOKEN_TILE_ROWS):
            ys_ref[pl.ds(k, tm, stride=TOKEN_TILE_ROWS), :] = _pack_pair(
                y[:, k * LANES:(k + 1) * LANES], y[:, half + k * LANES:half + (k + 1) * LANES])

    @pl.when(jnp.logical_not(valid))
    def _():
        ys_ref[...] = jnp.zeros(ys_ref.shape, ys_ref.dtype)


def _experts(tables, xs, w13, w2, layer):
    nblk = tables[0].shape[0]
    rows = xs.shape[0] // nblk
    blk = pl.BlockSpec((rows, LANES), lambda i, *_: (i, 0))
    return pl.pallas_call(
        functools.partial(_experts_kernel, layer=layer),
        grid_spec=pltpu.PrefetchScalarGridSpec(
            num_scalar_prefetch=len(tables),
            grid=(nblk,),
            in_specs=[blk, pl.BlockSpec(memory_space=pl.ANY), pl.BlockSpec(memory_space=pl.ANY)],
            out_specs=blk,
            scratch_shapes=[pltpu.VMEM((2, D_MODEL, 2 * EXPERT_FF), F32),
                            pltpu.VMEM((2, EXPERT_FF, D_MODEL), F32),
                            pltpu.VMEM((D_MODEL, 2 * EXPERT_FF), BF16),
                            pltpu.VMEM((EXPERT_FF, D_MODEL), BF16),
                            pltpu.SemaphoreType.DMA((2,)), pltpu.SemaphoreType.DMA((2,))],
        ),
        out_shape=jax.ShapeDtypeStruct(xs.shape, U32),
        compiler_params=_params("arbitrary"),
        name="experts",
    )(*tables, xs, w13, w2)


def _combine_kernel(slot_ref, x_ref, route_ref, ys_hbm, g_ref, o_ref, buf, sem, *, final_norm):
    tm = x_ref.shape[0]
    n = slot_ref.shape[2]
    rows = TOKEN_TILE_ROWS
    half = D_MODEL // 2

    def copy(src_row, dst_row):
        return pltpu.make_async_copy(ys_hbm.at[pl.ds(src_row, rows), :], buf.at[pl.ds(dst_row, rows), :], sem)

    def start(t, c):
        for k in range(2):
            a = k * tm + t
            src = pl.multiple_of(slot_ref[0, 0, a] * rows, rows)
            copy(src, pl.multiple_of(a * rows, rows)).start(priority=k)
        return c
    lax.fori_loop(0, tm, start, 0, unroll=DMA_UNROLL)

    def wait(a, c):
        copy(0, 0).wait()
        return c
    lax.fori_loop(0, n, wait, 0, unroll=2 * DMA_UNROLL)

    g1 = route_ref[:, 2:3]
    g2 = route_ref[:, 3:4]
    ss = jnp.zeros((tm, 1), F32)
    for k in range(rows):
        lo1, hi1 = _unpack_pair(buf[pl.ds(k, tm, stride=rows), :])
        lo2, hi2 = _unpack_pair(buf[pl.ds(tm * rows + k, tm, stride=rows), :])
        c_lo = slice(k * LANES, (k + 1) * LANES)
        c_hi = slice(half + k * LANES, half + (k + 1) * LANES)
        x_lo = x_ref[:, c_lo] + (g1 * lo1 + g2 * lo2)
        x_hi = x_ref[:, c_hi] + (g1 * hi1 + g2 * hi2)
        o_ref[:, c_lo] = x_lo
        o_ref[:, c_hi] = x_hi
        if final_norm:
            ss = (ss + jnp.sum(x_lo * x_lo, axis=-1, keepdims=True)
                  + jnp.sum(x_hi * x_hi, axis=-1, keepdims=True))
    if final_norm:
        scale = lax.rsqrt(ss * (1.0 / D_MODEL) + RMS_EPS)
        o_ref[...] = o_ref[...] * scale * g_ref[...]


def _combine(slot3, x2, route, ys, final_g, final_norm):
    t = x2.shape[0]
    nt, _, n = slot3.shape
    tm = n // 2
    row = lambda w: pl.BlockSpec((tm, w), lambda i: (i, 0))
    return pl.pallas_call(
        functools.partial(_combine_kernel, final_norm=final_norm),
        grid=(nt,),
        in_specs=[pl.BlockSpec((1, 1, n), lambda i: (i, 0, 0), memory_space=pltpu.SMEM),
                  row(D_MODEL), row(ROUTE_LANES), pl.BlockSpec(memory_space=pl.ANY),
                  _resident((1, D_MODEL))],
        out_specs=row(D_MODEL),
        out_shape=jax.ShapeDtypeStruct((t, D_MODEL), F32),
        scratch_shapes=[pltpu.VMEM((n * TOKEN_TILE_ROWS, LANES), U32), pltpu.SemaphoreType.DMA],
        compiler_params=_params("arbitrary"),
        name="combine",
    )(slot3, x2, route, ys, final_g)


def _dispatch_tables(route, counts, t):
    tm = MOE_ROWS
    ids = route[:, 0:2].astype(I32)
    rank = route[:, 4:6].astype(I32)
    counts = counts[0, 0:N_EXPERTS].astype(I32)
    padded = (counts + tm - 1) // tm * tm
    pad_end = jnp.cumsum(padded)
    pad_start = pad_end - padded
    experts = jnp.arange(N_EXPERTS, dtype=I32)
    slot = rank + jnp.sum(jnp.where(ids[..., None] == experts, pad_start, 0), axis=-1)
    nblk = 2 * t // tm + N_EXPERTS
    blk_row = jnp.arange(nblk, dtype=I32) * tm
    blk_valid = blk_row < pad_end[-1]
    be = jnp.minimum(jnp.sum((blk_row[:, None] >= pad_end[None, :]).astype(I32), axis=1), N_EXPERTS - 1)
    onehot = (be[:, None] == experts[None, :]).astype(I32)
    pick = lambda table: jnp.sum(onehot * table[None, :], axis=1)
    blk_e = jnp.where(blk_valid, be, -1)
    blk_off = blk_row - pick(pad_start)
    blk_first = (blk_valid & (blk_off == 0)).astype(I32)
    blk_cnt = jnp.where(blk_valid, jnp.clip(pick(counts) - blk_off, 0, tm), 0)
    used = counts > 0
    order = jnp.cumsum(used.astype(I32)) - 1
    cand = jnp.where(used[None, :] & (experts[None, :] > experts[:, None]), experts[None, :], N_EXPERTS)
    nxt = jnp.min(cand, axis=1)
    nxt = jnp.where(nxt >= N_EXPERTS, -1, nxt)
    tables = (blk_e, blk_first, pick(order) % 2, pick(nxt), blk_cnt)
    return slot, tuple(tb.astype(I32) for tb in tables)


def _moe(x2, hp, route, counts, w13, w2, layer, final_g, final_norm):
    t = x2.shape[0]
    slot, tables = _dispatch_tables(route, counts, t)
    td = min(DISPATCH_ROWS, t)
    nblk = tables[0].shape[0]
    xs_init = jnp.zeros((nblk * MOE_ROWS * TOKEN_TILE_ROWS, LANES), U32)
    xs = _dispatch(slot.reshape(t // td, 1, 2 * td), hp, xs_init)
    ys = _experts(tables, xs, w13, w2, layer)
    tc = min(COMBINE_ROWS, t)
    slot3 = slot.reshape(t // tc, tc, 2).transpose(0, 2, 1).reshape(t // tc, 1, 2 * tc)
    return _combine(slot3, x2, route, ys, final_g, final_norm)


def _rope_tables(positions):
    inv_freq = ROPE_THETA ** (-jnp.arange(0, HEAD_DIM, 2, dtype=F32) / HEAD_DIM)
    ang = positions.astype(F32)[..., None] * inv_freq
    c, s = jnp.cos(ang), jnp.sin(ang)
    t = positions.size
    cos128 = jnp.concatenate([c, c, c, c], axis=-1).reshape(t, LANES)
    sin128 = jnp.concatenate([-s, s, -s, s], axis=-1).reshape(t, LANES)
    return cos128, sin128


def _router_weights(wg, we, bg, be):
    w = jnp.zeros((D_MODEL, ROUTE_LANES), F32).at[:, 0:N_GROUPS].set(wg)
    w = w.at[:, N_GROUPS:N_GROUPS + N_EXPERTS].set(we)
    head = w.astype(BF16)
    tail = (w - head.astype(F32)).astype(BF16)
    bias = jnp.zeros((1, ROUTE_LANES), F32).at[0, 0:N_GROUPS].set(bg)
    bias = bias.at[0, N_GROUPS:N_GROUPS + N_EXPERTS].set(be)
    return jnp.concatenate([head, tail], axis=1), bias


def kernel(x, positions, attn_norm_g, w_in, conv_dw_w, conv_dw_b, conv_ln_g, conv_ln_b, conv_pw_w, conv_out_g, swa_sinks, swa_out_g, diff_lambda_q1, diff_lambda_k1, diff_lambda_q2, diff_lambda_k2, diff_subln_g, w_out, ffn_norm_g, router_group_w, router_group_b, router_expert_w, router_expert_b, moe_w13, moe_w2, final_norm_g):
    b, s, d = x.shape
    t = b * s
    depth = w_in.shape[0]
    cos128, sin128 = _rope_tables(positions)
    x2 = x.reshape(t, d)
    row = lambda v: v.reshape(1, -1)
    for i in range(depth):
        lambda_init = 0.8 - 0.6 * float(np.exp(-0.3 * i))
        z, qs, ks, vs, qd, kd, vd = _inproj(x2, row(attn_norm_g[i]), w_in[i].astype(BF16), cos128, sin128)
        conv_o = _conv(z.reshape(b, s, CONV_CH), conv_dw_w[i], row(conv_dw_b[i]), row(conv_ln_g[i]),
                       row(conv_ln_b[i]), conv_pw_w[i].astype(BF16), row(conv_out_g[i]))
        swa_o = _swa(qs.reshape(b, s, -1), ks.reshape(b, s, -1), vs.reshape(b, s, -1),
                     row(swa_sinks[i]), row(swa_out_g[i]))
        diff_o = _diff(qd.reshape(b, s, -1), kd.reshape(b, s, -1), vd.reshape(b, s, -1),
                       row(diff_lambda_q1[i]), row(diff_lambda_k1[i]), row(diff_lambda_q2[i]),
                       row(diff_lambda_k2[i]), row(diff_subln_g[i]), lambda_init)
        wr, br = _router_weights(router_group_w[i], router_expert_w[i], router_group_b[i], router_expert_b[i])
        x2, hp, route, counts = _outproj(x2, conv_o.reshape(t, -1), swa_o.reshape(t, -1),
                                         diff_o.reshape(t, -1), w_out[i].astype(BF16), row(ffn_norm_g[i]),
                                         wr, br)
        x2 = _moe(x2, hp, route, counts, moe_w13, moe_w2, i, row(final_norm_g), i == depth - 1)
    return x2.reshape(b, s, d)
```

```python
import functools

import numpy as np
import jax
import jax.numpy as jnp
from jax import lax
from jax.experimental import pallas as pl
from jax.experimental.pallas import tpu as pltpu

F32 = jnp.float32
BF16 = jnp.bfloat16
I32 = jnp.int32
U32 = jnp.uint32

D_MODEL = 2048
HEAD_DIM = 64
CONV_CH = 512
CONV_WIDTH = 31
SWA_Q_HEADS = 12
SWA_KV_HEADS = 4
SWA_GROUP = SWA_Q_HEADS // SWA_KV_HEADS
SWA_WINDOW = 128
SWA_WIDTH = SWA_Q_HEADS * HEAD_DIM
SWA_KV_WIDTH = SWA_KV_HEADS * HEAD_DIM
DIFF_HEADS = 6
DIFF_WIDTH = DIFF_HEADS * 2 * HEAD_DIM
IN_WIDTH = 2 * CONV_CH + SWA_WIDTH + 2 * SWA_KV_WIDTH + 3 * DIFF_WIDTH
ROPE_THETA = 10000.0
N_GROUPS = 8
EXPERTS_PER_GROUP = 8
N_EXPERTS = N_GROUPS * EXPERTS_PER_GROUP
EXPERT_FF = 512
RMS_EPS = 1e-6
LN_EPS = 1e-5
NEG_INF = -1e30
LOG2_E = 1.4426950408889634

LANES = 128
SUBLANES = 8
VMEM_LIMIT_BYTES = 56 * 1024 * 1024

INPROJ_ROWS = 512
CONV_ROWS = 256
CONV_HALO = 32
DIFF_ROWS = 512
OUTPROJ_ROWS = 512
OUTPROJ_SPLIT = 1
MOE_ROWS = 128
DISPATCH_ROWS = 256
COMBINE_ROWS = 256
TOKEN_TILE_ROWS = D_MODEL // 2 // LANES
DMA_UNROLL = 8
W13_CHUNKS = 4
W2_CHUNKS = 2
ROUTE_LANES = 128


def _params(*sem):
    return pltpu.CompilerParams(dimension_semantics=sem, vmem_limit_bytes=VMEM_LIMIT_BYTES)


def _resident(shape):
    nd = len(shape)
    return pl.BlockSpec(shape, lambda *_: (0,) * nd, pipeline_mode=pl.Buffered(1))


def _rope(r, cos, sin_signed, first_half):
    outs = []
    for k in range(r.shape[1] // LANES):
        seg = r[:, k * LANES:(k + 1) * LANES]
        partner = jnp.where(first_half,
                            pltpu.roll(seg, LANES - HEAD_DIM // 2, 1),
                            pltpu.roll(seg, HEAD_DIM // 2, 1))
        outs.append(seg * cos + partner * sin_signed)
    return outs[0] if len(outs) == 1 else jnp.concatenate(outs, axis=1)


def _dup_heads(r, low_head):
    outs = []
    for k in range(r.shape[1] // LANES):
        seg = r[:, k * LANES:(k + 1) * LANES]
        rot = pltpu.roll(seg, HEAD_DIM, 1)
        outs.append(jnp.where(low_head, seg, rot))
        outs.append(jnp.where(low_head, rot, seg))
    return jnp.concatenate(outs, axis=1)


def _inproj_kernel(x_ref, g_ref, w_ref, cos_ref, sin_ref,
                   z_ref, qs_ref, ks_ref, vs_ref, qd_ref, kd_ref, vd_ref):
    x = x_ref[...]
    ms = jnp.mean(x * x, axis=-1, keepdims=True)
    h = (x * lax.rsqrt(ms + RMS_EPS) * g_ref[...]).astype(BF16)
    cos = cos_ref[...]
    sin_signed = sin_ref[...]
    lane = lax.broadcasted_iota(I32, cos.shape, 1)
    first_half = (lane & (HEAD_DIM // 2)) == 0
    nc = 2 * LANES
    q_scale = HEAD_DIM ** -0.5

    def proj(c0):
        return jnp.dot(h, w_ref[:, c0:c0 + nc], preferred_element_type=F32)

    for j in range(CONV_CH // nc):
        a = proj(j * nc)
        gate = proj(CONV_CH + j * nc)
        z_ref[:, j * nc:(j + 1) * nc] = a * jax.nn.sigmoid(gate)
    base = 2 * CONV_CH
    for j in range(SWA_WIDTH // nc):
        r = _rope(proj(base + j * nc), cos, sin_signed, first_half)
        qs_ref[:, j * nc:(j + 1) * nc] = (r * q_scale).astype(BF16)
    base += SWA_WIDTH
    low_head = lane < HEAD_DIM
    for j in range(SWA_KV_WIDTH // nc):
        r = _rope(proj(base + j * nc), cos, sin_signed, first_half)
        ks_ref[:, 2 * j * nc:2 * (j + 1) * nc] = _dup_heads(r, low_head).astype(BF16)
    base += SWA_KV_WIDTH
    for j in range(SWA_KV_WIDTH // nc):
        vs_ref[:, 2 * j * nc:2 * (j + 1) * nc] = _dup_heads(proj(base + j * nc), low_head).astype(BF16)
    base += SWA_KV_WIDTH
    for j in range(DIFF_WIDTH // nc):
        r = _rope(proj(base + j * nc), cos, sin_signed, first_half)
        qd_ref[:, j * nc:(j + 1) * nc] = (r * (q_scale * LOG2_E)).astype(BF16)
    base += DIFF_WIDTH
    for j in range(DIFF_WIDTH // nc):
        r = _rope(proj(base + j * nc), cos, sin_signed, first_half)
        kd_ref[:, j * nc:(j + 1) * nc] = r.astype(BF16)
    base += DIFF_WIDTH
    for j in range(DIFF_WIDTH // nc):
        vd_ref[:, j * nc:(j + 1) * nc] = proj(base + j * nc).astype(BF16)


def _inproj(x2, g, w_bf, cos128, sin128):
    t = x2.shape[0]
    tm = min(INPROJ_ROWS, t)
    row = lambda w: pl.BlockSpec((tm, w), lambda i: (i, 0))
    widths = (CONV_CH, SWA_WIDTH, 2 * SWA_KV_WIDTH, 2 * SWA_KV_WIDTH, DIFF_WIDTH, DIFF_WIDTH, DIFF_WIDTH)
    dtypes = (F32,) + (BF16,) * 6
    return pl.pallas_call(
        _inproj_kernel,
        grid=(t // tm,),
        in_specs=[row(D_MODEL), _resident((1, D_MODEL)), _resident((D_MODEL, IN_WIDTH)),
                  row(LANES), row(LANES)],
        out_specs=[row(w) for w in widths],
        out_shape=[jax.ShapeDtypeStruct((t, w), dt) for w, dt in zip(widths, dtypes)],
        compiler_params=_params("parallel"),
        name="inproj",
    )(x2, g, w_bf, cos128, sin128)


def _conv_kernel(z_ref, halo_ref, dww_ref, dwb_ref, lng_ref, lnb_ref, pw_ref, og_ref,
                 o_ref, buf_ref, y_ref):
    ts = z_ref.shape[1]
    i = pl.program_id(1)
    buf_ref[0, 0:CONV_HALO, :] = jnp.where(i > 0, halo_ref[0], 0.0)
    buf_ref[0, CONV_HALO:CONV_HALO + ts, :] = z_ref[0]
    kept = CONV_HALO + ts - SUBLANES
    for b in range(1, SUBLANES):
        buf_ref[b, 0:kept, :] = buf_ref[0, b:b + kept, :]
    rows = 64
    first = CONV_HALO - (CONV_WIDTH - 1)
    for c in range(CONV_CH // LANES):
        cs = slice(c * LANES, (c + 1) * LANES)
        for r in range(ts // rows):
            acc = jnp.broadcast_to(dwb_ref[:, cs], (rows, LANES))
            for j in range(CONV_WIDTH):
                b = (first + j) % SUBLANES
                s0 = first + j - b + r * rows
                acc = acc + buf_ref[b, s0:s0 + rows, cs] * dww_ref[j:j + 1, cs]
            y_ref[r * rows:(r + 1) * rows, cs] = acc
    y = y_ref[...]
    mu = jnp.mean(y, axis=-1, keepdims=True)
    yc = y - mu
    var = jnp.mean(yc * yc, axis=-1, keepdims=True)
    yn = yc * lax.rsqrt(var + LN_EPS) * lng_ref[...] + lnb_ref[...]
    act = yn * jax.nn.sigmoid(yn)
    p = jnp.dot(act.astype(BF16), pw_ref[...], preferred_element_type=F32)
    ms = jnp.mean(p * p, axis=-1, keepdims=True)
    o_ref[0] = (p * lax.rsqrt(ms + RMS_EPS) * og_ref[...]).astype(BF16)


def _conv(z3, dw_w, dw_b, ln_g, ln_b, pw_bf, out_g):
    b, s, _ = z3.shape
    ts = min(CONV_ROWS, s)
    hb = ts // CONV_HALO
    return pl.pallas_call(
        _conv_kernel,
        grid=(b, s // ts),
        in_specs=[pl.BlockSpec((1, ts, CONV_CH), lambda bi, i: (bi, i, 0)),
                  pl.BlockSpec((1, CONV_HALO, CONV_CH), lambda bi, i: (bi, jnp.maximum(i * hb - 1, 0), 0)),
                  _resident((CONV_WIDTH, CONV_CH)), _resident((1, CONV_CH)), _resident((1, CONV_CH)),
                  _resident((1, CONV_CH)), _resident((CONV_CH, CONV_CH)), _resident((1, CONV_CH))],
        out_specs=pl.BlockSpec((1, ts, CONV_CH), lambda bi, i: (bi, i, 0)),
        out_shape=jax.ShapeDtypeStruct((b, s, CONV_CH), BF16),
        scratch_shapes=[pltpu.VMEM((SUBLANES, CONV_HALO + ts, CONV_CH), F32), pltpu.VMEM((ts, CONV_CH), F32)],
        compiler_params=_params("parallel", "arbitrary"),
        name="conv",
    )(z3, z3, dw_w, dw_b, ln_g, ln_b, pw_bf, out_g)


def _swa_kernel(q_ref, kc_ref, kp_ref, vc_ref, vp_ref, sink_ref, g_ref, o_ref):
    w = SWA_WINDOW
    n = pl.program_id(1)
    rows = SWA_GROUP * w
    qi = lax.broadcasted_iota(I32, (rows, 2 * w), 0) & (w - 1)
    kj = lax.broadcasted_iota(I32, (rows, 2 * w), 1)
    valid = ((kj < w) & (kj > qi) & (n > 0)) | ((kj >= w) & (kj - w <= qi))
    low_head = lax.broadcasted_iota(I32, (w, LANES), 1) < HEAD_DIM
    contract = (((1,), (1,)), ((), ()))
    zero = jnp.zeros((w, LANES), BF16)
    heads = [None] * SWA_Q_HEADS
    for hk in range(SWA_KV_HEADS):
        cs = slice(hk * LANES, (hk + 1) * LANES)
        kk = jnp.concatenate([kp_ref[0, :, cs], kc_ref[0, :, cs]], axis=0)
        vv = jnp.concatenate([vp_ref[0, :, cs], vc_ref[0, :, cs]], axis=0)
        q_parts, sink_parts = [], []
        for gq in range(SWA_GROUP):
            hq = hk * SWA_GROUP + gq
            tile = q_ref[0, :, (hq // 2) * LANES:(hq // 2 + 1) * LANES]
            q_parts.append(jnp.where(low_head, tile, zero) if hq % 2 == 0 else jnp.where(low_head, zero, tile))
            sink_parts.append(jnp.broadcast_to(sink_ref[:, hq:hq + 1], (w, LANES)))
        q3 = jnp.concatenate(q_parts, axis=0)
        sink = jnp.concatenate(sink_parts, axis=0)
        s = lax.dot_general(q3, kk, contract, preferred_element_type=F32)
        s = jnp.where(valid, s, NEG_INF)
        m = jnp.maximum(jnp.max(s, axis=-1, keepdims=True), sink)
        e = jnp.exp(s - jnp.concatenate([m, m], axis=1))
        den = jnp.sum(e, axis=-1, keepdims=True) + jnp.exp(sink - m)
        o = jnp.dot(e.astype(BF16), vv, preferred_element_type=F32) * (1.0 / den)
        for gq in range(SWA_GROUP):
            heads[hk * SWA_GROUP + gq] = o[gq * w:(gq + 1) * w]
    tiles = []
    ss = jnp.zeros((w, 1), F32)
    for t in range(SWA_Q_HEADS // 2):
        tile = jnp.where(low_head, heads[2 * t], heads[2 * t + 1])
        ss = ss + jnp.sum(tile * tile, axis=-1, keepdims=True)
        tiles.append(tile)
    scale = lax.rsqrt(ss * (1.0 / SWA_WIDTH) + RMS_EPS)
    o_ref[0] = (jnp.concatenate(tiles, axis=1) * scale * g_ref[...]).astype(BF16)


def _swa(q3, k3, v3, sinks, out_g):
    b, s, _ = q3.shape
    w = SWA_WINDOW
    kvw = k3.shape[2]
    cur = lambda width: pl.BlockSpec((1, w, width), lambda bi, n: (bi, n, 0))
    prev = lambda width: pl.BlockSpec((1, w, width), lambda bi, n: (bi, jnp.maximum(n - 1, 0), 0))
    return pl.pallas_call(
        _swa_kernel,
        grid=(b, s // w),
        in_specs=[cur(SWA_WIDTH), cur(kvw), prev(kvw), cur(kvw), prev(kvw),
                  _resident((1, SWA_Q_HEADS)), _resident((1, SWA_WIDTH))],
        out_specs=cur(SWA_WIDTH),
        out_shape=jax.ShapeDtypeStruct((b, s, SWA_WIDTH), BF16),
        compiler_params=_params("parallel", "arbitrary"),
        name="swa",
    )(q3, k3, k3, v3, v3, sinks, out_g)


def _diff_kernel(q_ref, k_ref, v_ref, bias_ref, lq1_ref, lk1_ref, lq2_ref, lk2_ref, g_ref, o_ref,
                 q2_ref, m_ref, l_ref, acc_ref, *, lambda_init):
    tq = q_ref.shape[1]
    tk = tq
    i = pl.program_id(2)
    contract = (((1,), (1,)), ((), ()))
    q = q_ref[0]
    low_head = lax.broadcasted_iota(I32, q.shape, 1) < HEAD_DIM
    zero = jnp.zeros(q.shape, q.dtype)
    q2_ref[0:tq, :] = jnp.where(low_head, q, zero)
    q2_ref[tq:2 * tq, :] = jnp.where(low_head, zero, q)
    m_ref[...] = jnp.full(m_ref.shape, NEG_INF, F32)
    l_ref[...] = jnp.zeros(l_ref.shape, F32)
    acc_ref[...] = jnp.zeros(acc_ref.shape, F32)
    reps = tk // LANES

    def step(j, masked):
        k0 = pl.multiple_of(j * tk, tk)
        kblk = k_ref[0, pl.ds(k0, tk), :]
        vblk = v_ref[0, pl.ds(k0, tk), :]
        s = lax.dot_general(q2_ref[...], kblk, contract, preferred_element_type=F32)
        if masked:
            s = s + bias_ref[...]
        m_prev = m_ref[...]
        m_next = jnp.maximum(m_prev, jnp.max(s, axis=-1, keepdims=True))
        alpha = jnp.exp2(m_prev - m_next)
        p = jnp.exp2(s - jnp.concatenate([m_next] * reps, axis=1))
        l_ref[...] = alpha * l_ref[...] + jnp.sum(p, axis=-1, keepdims=True)
        acc_ref[...] = alpha * acc_ref[...] + jnp.dot(p.astype(BF16), vblk, preferred_element_type=F32)
        m_ref[...] = m_next

    def body(j, carry):
        step(j, False)
        return carry

    lax.fori_loop(0, i, body, 0)
    step(i, True)

    lam = (jnp.exp(jnp.sum(lq1_ref[...] * lk1_ref[...], axis=-1, keepdims=True))
           - jnp.exp(jnp.sum(lq2_ref[...] * lk2_ref[...], axis=-1, keepdims=True)) + lambda_init)
    o = acc_ref[0:tq, :] / l_ref[0:tq, :] - lam * (acc_ref[tq:2 * tq, :] / l_ref[tq:2 * tq, :])
    ms = jnp.mean(o * o, axis=-1, keepdims=True)
    o_ref[0] = (o * lax.rsqrt(ms + LN_EPS) * g_ref[...] * (1.0 - lambda_init)).astype(BF16)


def _diff(q3, k3, v3, lq1, lk1, lq2, lk2, subln_g, lambda_init):
    b, s, _ = q3.shape
    tq = min(DIFF_ROWS, s)
    hw = 2 * HEAD_DIM
    qspec = pl.BlockSpec((1, tq, hw), lambda bi, h, i: (bi, i, h))
    kvspec = pl.BlockSpec((1, s, hw), lambda bi, h, i: (bi, 0, h))
    vec = _resident((1, HEAD_DIM))
    qrow = lax.broadcasted_iota(I32, (2, tq, tq), 1).reshape(2 * tq, tq)
    kcol = lax.broadcasted_iota(I32, (2 * tq, tq), 1)
    bias = jnp.where(kcol <= qrow, 0.0, NEG_INF).astype(F32)
    return pl.pallas_call(
        functools.partial(_diff_kernel, lambda_init=lambda_init),
        grid=(b, DIFF_HEADS, s // tq),
        in_specs=[qspec, kvspec, kvspec, _resident((2 * tq, tq)), vec, vec, vec, vec, _resident((1, hw))],
        out_specs=qspec,
        out_shape=jax.ShapeDtypeStruct((b, s, DIFF_WIDTH), BF16),
        scratch_shapes=[pltpu.VMEM((2 * tq, hw), BF16), pltpu.VMEM((2 * tq, hw), F32),
                        pltpu.VMEM((2 * tq, hw), F32), pltpu.VMEM((2 * tq, hw), F32)],
        compiler_params=_params("parallel", "parallel", "arbitrary"),
        name="diff",
    )(q3, k3, v3, bias, lq1, lk1, lq2, lk2, subln_g)


def _bits(v):
    return lax.bitcast_convert_type(v, U32)


def _route(logits):
    lane = lax.broadcasted_iota(I32, logits.shape, 1)
    big = jnp.int32(ROUTE_LANES)
    is_group = lane < N_GROUPS
    gmax = jnp.max(jnp.where(is_group, logits, NEG_INF), axis=-1, keepdims=True)
    g_idx = jnp.min(jnp.where(is_group & (logits == gmax), lane, big), axis=-1, keepdims=True)
    g_w = 1.0 / jnp.sum(jnp.where(is_group, jnp.exp(logits - gmax), 0.0), axis=-1, keepdims=True)
    lo = N_GROUPS + g_idx * EXPERTS_PER_GROUP
    in_sel = (lane >= lo) & (lane < lo + EXPERTS_PER_GROUP)
    e1 = jnp.max(jnp.where(in_sel, logits, NEG_INF), axis=-1, keepdims=True)
    i1 = jnp.min(jnp.where(in_sel & (logits == e1), lane, big), axis=-1, keepdims=True)
    rest = in_sel & (lane != i1)
    e2 = jnp.max(jnp.where(rest, logits, NEG_INF), axis=-1, keepdims=True)
    i2 = jnp.min(jnp.where(rest & (logits == e2), lane, big), axis=-1, keepdims=True)
    t = jnp.exp(e2 - e1)
    w1 = 1.0 / (1.0 + t)
    w2 = t * w1
    out = jnp.where(lane == 0, (i1 - N_GROUPS).astype(F32),
          jnp.where(lane == 1, (i2 - N_GROUPS).astype(F32),
          jnp.where(lane == 2, g_w * w1,
          jnp.where(lane == 3, g_w * w2, 0.0))))
    return out


def _pack_pair(lo_f32, hi_f32):
    lo = _bits(lo_f32.astype(BF16).astype(F32)) >> 16
    hi = _bits(hi_f32.astype(BF16).astype(F32)) & jnp.uint32(0xFFFF0000)
    return lo | hi


def _unpack_pair(words):
    lo = lax.bitcast_convert_type(words << 16, F32)
    hi = lax.bitcast_convert_type(words & jnp.uint32(0xFFFF0000), F32)
    return lo, hi


def _outproj_kernel(x_ref, c_ref, s_ref, d_ref, w_ref, g_ref, wr_ref, br_ref,
                    xo_ref, hp_ref, route_ref, counts_ref, run_ref):
    tm = x_ref.shape[0]

    @pl.when(pl.program_id(0) == 0)
    def _():
        run_ref[...] = jnp.zeros(run_ref.shape, F32)

    nc = 4 * LANES
    o1, o2 = CONV_CH, CONV_CH + SWA_WIDTH
    half = D_MODEL // 2
    sub = tm // OUTPROJ_SPLIT
    tri = (lax.broadcasted_iota(I32, (sub, sub), 1) < lax.broadcasted_iota(I32, (sub, sub), 0)).astype(BF16)
    for r0 in range(0, tm, sub):
        rs = slice(r0, r0 + sub)
        c_in, s_in, d_in = c_ref[rs, :], s_ref[rs, :], d_ref[rs, :]
        ss = jnp.zeros((sub, 1), F32)
        for j in range(D_MODEL // nc):
            cs = slice(j * nc, (j + 1) * nc)
            y = (jnp.dot(c_in, w_ref[0:o1, cs], preferred_element_type=F32)
                 + jnp.dot(s_in, w_ref[o1:o2, cs], preferred_element_type=F32)
                 + jnp.dot(d_in, w_ref[o2:D_MODEL, cs], preferred_element_type=F32))
            xn = x_ref[rs, cs] + y
            xo_ref[rs, cs] = xn
            ss = ss + jnp.sum(xn * xn, axis=-1, keepdims=True)
        scale = lax.rsqrt(ss * (1.0 / D_MODEL) + RMS_EPS)
        h_lo = xo_ref[rs, 0:half] * scale * g_ref[:, 0:half]
        h_hi = xo_ref[rs, half:D_MODEL] * scale * g_ref[:, half:D_MODEL]
        words = _pack_pair(h_lo, h_hi)
        for k in range(TOKEN_TILE_ROWS):
            hp_ref[pl.ds(r0 * TOKEN_TILE_ROWS + k, sub, stride=TOKEN_TILE_ROWS), :] = (
                words[:, k * LANES:(k + 1) * LANES])
        b_lo, b_hi = h_lo.astype(BF16), h_hi.astype(BF16)
        r_lo = (h_lo - b_lo.astype(F32)).astype(BF16)
        r_hi = (h_hi - b_hi.astype(F32)).astype(BF16)
        acc = (jnp.dot(b_lo, wr_ref[0:half, :], preferred_element_type=F32)
               + jnp.dot(b_hi, wr_ref[half:D_MODEL, :], preferred_element_type=F32)
               + jnp.dot(r_lo, wr_ref[0:half, :], preferred_element_type=F32)
               + jnp.dot(r_hi, wr_ref[half:D_MODEL, :], preferred_element_type=F32))
        logits = acc[:, 0:ROUTE_LANES] + acc[:, ROUTE_LANES:2 * ROUTE_LANES] + br_ref[...]
        route = _route(logits)
        lane = lax.broadcasted_iota(I32, route.shape, 1)
        hot1 = (lane == route[:, 0:1].astype(I32)).astype(F32)
        hot2 = (lane == route[:, 1:2].astype(I32)).astype(F32)
        hot = hot1 + hot2
        before = jnp.dot(tri, hot.astype(BF16), preferred_element_type=F32) + run_ref[...]
        rank1 = jnp.sum(hot1 * before, axis=-1, keepdims=True)
        rank2 = jnp.sum(hot2 * before, axis=-1, keepdims=True)
        route_ref[rs, :] = jnp.where(lane == 4, rank1, jnp.where(lane == 5, rank2, route))
        run_ref[...] = run_ref[...] + jnp.sum(hot, axis=0, keepdims=True)
    counts_ref[...] = run_ref[...]


def _outproj(x2, conv_o, swa_o, diff_o, w_bf, ffn_g, wr_bf, br):
    t = x2.shape[0]
    tm = min(OUTPROJ_ROWS, t)
    row = lambda w: pl.BlockSpec((tm, w), lambda i: (i, 0))
    return pl.pallas_call(
        _outproj_kernel,
        grid=(t // tm,),
        in_specs=[row(D_MODEL), row(CONV_CH), row(SWA_WIDTH), row(DIFF_WIDTH),
                  _resident((D_MODEL, D_MODEL)), _resident((1, D_MODEL)),
                  _resident((D_MODEL, 2 * ROUTE_LANES)), _resident((1, ROUTE_LANES))],
        out_specs=[row(D_MODEL), pl.BlockSpec((tm * TOKEN_TILE_ROWS, LANES), lambda i: (i, 0)),
                   row(ROUTE_LANES), pl.BlockSpec((1, ROUTE_LANES), lambda i: (0, 0))],
        out_shape=[jax.ShapeDtypeStruct((t, D_MODEL), F32),
                   jax.ShapeDtypeStruct((t * TOKEN_TILE_ROWS, LANES), U32),
                   jax.ShapeDtypeStruct((t, ROUTE_LANES), F32),
                   jax.ShapeDtypeStruct((1, ROUTE_LANES), F32)],
        scratch_shapes=[pltpu.VMEM((1, ROUTE_LANES), F32)],
        compiler_params=_params("arbitrary"),
        name="outproj",
    )(x2, conv_o, swa_o, diff_o, w_bf, ffn_g, wr_bf, br)


def _dispatch_kernel(slot_ref, hp_ref, init_ref, xs_ref, sem):
    del init_ref
    n = slot_ref.shape[2]
    rows = TOKEN_TILE_ROWS

    def copy(src_row, dst_row):
        return pltpu.make_async_copy(hp_ref.at[pl.ds(src_row, rows), :], xs_ref.at[pl.ds(dst_row, rows), :], sem)

    def start(t, c):
        src = pl.multiple_of(t * rows, rows)
        for k in range(2):
            dst = pl.multiple_of(slot_ref[0, 0, 2 * t + k] * rows, rows)
            copy(src, dst).start(priority=k)
        return c
    lax.fori_loop(0, n // 2, start, 0, unroll=DMA_UNROLL)

    def wait(a, c):
        copy(0, 0).wait()
        return c
    lax.fori_loop(0, n, wait, 0, unroll=2 * DMA_UNROLL)


def _dispatch(slot3, hp, xs_init):
    nt, _, n = slot3.shape
    td = n // 2
    return pl.pallas_call(
        _dispatch_kernel,
        grid=(nt,),
        in_specs=[pl.BlockSpec((1, 1, n), lambda i: (i, 0, 0), memory_space=pltpu.SMEM),
                  pl.BlockSpec((td * TOKEN_TILE_ROWS, LANES), lambda i: (i, 0)),
                  pl.BlockSpec(memory_space=pl.ANY)],
        out_specs=pl.BlockSpec(memory_space=pl.ANY),
        out_shape=jax.ShapeDtypeStruct(xs_init.shape, xs_init.dtype),
        scratch_shapes=[pltpu.SemaphoreType.DMA],
        input_output_aliases={2: 0},
        compiler_params=_params("arbitrary"),
        name="dispatch",
    )(slot3, hp, xs_init)


def _experts_kernel(be_ref, bfirst_ref, bslot_ref, bnext_ref, bnext2_ref, bcnt_ref, nvalid_ref,
                    xs_ref, w13_hbm, w2_hbm, ys_ref,
                    w13_buf, w2_buf, w13_bf, w2_bf, sem13, sem2, *, layer):
    del nvalid_ref
    i = pl.program_id(0)
    e = be_ref[i]
    valid = e >= 0
    slot = bslot_ref[i]

    def w_copies(expert, s):
        copies = []
        r13 = D_MODEL // W13_CHUNKS
        for c in range(W13_CHUNKS):
            rs = pl.ds(c * r13, r13)
            copies.append(pltpu.make_async_copy(w13_hbm.at[layer, expert, rs, :], w13_buf.at[s, rs, :],
                                                sem13.at[s]))
        r2 = EXPERT_FF // W2_CHUNKS
        for c in range(W2_CHUNKS):
            rs = pl.ds(c * r2, r2)
            copies.append(pltpu.make_async_copy(w2_hbm.at[layer, expert, rs, :], w2_buf.at[s, rs, :],
                                                sem2.at[s]))
        return copies

    def start_all(copies):
        for c, cp in enumerate(copies):
            cp.start(priority=c % 2)

    @pl.when(valid & (bfirst_ref[i] == 1))
    def _():
        nxt = bnext_ref[i]
        nxt2 = bnext2_ref[i]

        @pl.when(i == 0)
        def _():
            start_all(w_copies(e, slot))

            @pl.when(nxt >= 0)
            def _():
                start_all(w_copies(nxt, 1 - slot))
        for cp in w_copies(e, slot):
            cp.wait()
        w13_bf[...] = w13_buf[slot].astype(BF16)
        w2_bf[...] = w2_buf[slot].astype(BF16)

        @pl.when(nxt2 >= 0)
        def _():
            start_all(w_copies(nxt2, slot))

    @pl.when(valid)
    def _():
        half = D_MODEL // 2
        tm = xs_ref.shape[0] // TOKEN_TILE_ROWS
        live = lax.broadcasted_iota(I32, (tm, LANES), 0) < bcnt_ref[i]
        los, his = [], []
        for k in range(TOKEN_TILE_ROWS):
            words = jnp.where(live, xs_ref[pl.ds(k, tm, stride=TOKEN_TILE_ROWS), :], jnp.uint32(0))
            lo, hi = _unpack_pair(words)
            los.append(lo.astype(BF16))
            his.append(hi.astype(BF16))
        x_lo = jnp.concatenate(los, axis=1)
        x_hi = jnp.concatenate(his, axis=1)
        h = (jnp.dot(x_lo, w13_bf[0:half, :], preferred_element_type=F32)
             + jnp.dot(x_hi, w13_bf[half:D_MODEL, :], preferred_element_type=F32))
        a, b = h[:, 0:EXPERT_FF], h[:, EXPERT_FF:2 * EXPERT_FF]
        act = (a * jax.nn.sigmoid(a) * b).astype(BF16)
        y = jnp.dot(act, w2_bf[...], preferred_element_type=F32)
        for k in range(TOKEN_TILE_ROWS):
            ys_ref[pl.ds(k, tm, stride=TOKEN_TILE_ROWS), :] = _pack_pair(
                y[:, k * LANES:(k + 1) * LANES], y[:, half + k * LANES:half + (k + 1) * LANES])


def _experts(tables, xs, w13, w2, layer):
    nblk = tables[0].shape[0]
    rows = xs.shape[0] // nblk
    blk = pl.BlockSpec((rows, LANES), lambda i, *tb: (jnp.minimum(i, tb[-1][0] - 1), 0))
    return pl.pallas_call(
        functools.partial(_experts_kernel, layer=layer),
        grid_spec=pltpu.PrefetchScalarGridSpec(
            num_scalar_prefetch=len(tables),
            grid=(nblk,),
            in_specs=[blk, pl.BlockSpec(memory_space=pl.ANY), pl.BlockSpec(memory_space=pl.ANY)],
            out_specs=blk,
            scratch_shapes=[pltpu.VMEM((2, D_MODEL, 2 * EXPERT_FF), F32),
                            pltpu.VMEM((2, EXPERT_FF, D_MODEL), F32),
                            pltpu.VMEM((D_MODEL, 2 * EXPERT_FF), BF16),
                            pltpu.VMEM((EXPERT_FF, D_MODEL), BF16),
                            pltpu.SemaphoreType.DMA((2,)), pltpu.SemaphoreType.DMA((2,))],
        ),
        out_shape=jax.ShapeDtypeStruct(xs.shape, U32),
        input_output_aliases={len(tables): 0},
        compiler_params=_params("arbitrary"),
        name="experts",
    )(*tables, xs, w13, w2)


def _combine_kernel(slot_ref, slot_next_ref, x_ref, route_ref, ys_hbm, g_ref, o_ref, buf, sem, *, final_norm):
    tm = x_ref.shape[0]
    n = slot_ref.shape[2]
    rows = TOKEN_TILE_ROWS
    half = D_MODEL // 2
    i = pl.program_id(0)
    cur = i % 2

    def copy(src_row, b, dst_row):
        return pltpu.make_async_copy(ys_hbm.at[pl.ds(src_row, rows), :], buf.at[b, pl.ds(dst_row, rows), :],
                                     sem.at[b])

    def request(slots, b):
        def start(t, c):
            for k in range(2):
                a = k * tm + t
                src = pl.multiple_of(slots[0, 0, a] * rows, rows)
                copy(src, b, pl.multiple_of(a * rows, rows)).start(priority=k)
            return c
        lax.fori_loop(0, tm, start, 0, unroll=DMA_UNROLL)

    @pl.when(i == 0)
    def _():
        request(slot_ref, 0)

    @pl.when(i + 1 < pl.num_programs(0))
    def _():
        request(slot_next_ref, 1 - cur)

    def wait(a, c):
        copy(0, cur, 0).wait()
        return c
    lax.fori_loop(0, n, wait, 0, unroll=2 * DMA_UNROLL)

    g1 = route_ref[:, 2:3]
    g2 = route_ref[:, 3:4]
    ss = jnp.zeros((tm, 1), F32)
    for k in range(rows):
        lo1, hi1 = _unpack_pair(buf[cur, pl.ds(k, tm, stride=rows), :])
        lo2, hi2 = _unpack_pair(buf[cur, pl.ds(tm * rows + k, tm, stride=rows), :])
        c_lo = slice(k * LANES, (k + 1) * LANES)
        c_hi = slice(half + k * LANES, half + (k + 1) * LANES)
        x_lo = x_ref[:, c_lo] + (g1 * lo1 + g2 * lo2)
        x_hi = x_ref[:, c_hi] + (g1 * hi1 + g2 * hi2)
        o_ref[:, c_lo] = x_lo
        o_ref[:, c_hi] = x_hi
        if final_norm:
            ss = (ss + jnp.sum(x_lo * x_lo, axis=-1, keepdims=True)
                  + jnp.sum(x_hi * x_hi, axis=-1, keepdims=True))
    if final_norm:
        scale = lax.rsqrt(ss * (1.0 / D_MODEL) + RMS_EPS)
        o_ref[...] = o_ref[...] * scale * g_ref[...]


def _combine(slot3, x2, route, ys, final_g, final_norm):
    t = x2.shape[0]
    nt, _, n = slot3.shape
    tm = n // 2
    row = lambda w: pl.BlockSpec((tm, w), lambda i: (i, 0))
    return pl.pallas_call(
        functools.partial(_combine_kernel, final_norm=final_norm),
        grid=(nt,),
        in_specs=[pl.BlockSpec((1, 1, n), lambda i: (i, 0, 0), memory_space=pltpu.SMEM),
                  pl.BlockSpec((1, 1, n), lambda i: (jnp.minimum(i + 1, nt - 1), 0, 0),
                               memory_space=pltpu.SMEM),
                  row(D_MODEL), row(ROUTE_LANES), pl.BlockSpec(memory_space=pl.ANY),
                  _resident((1, D_MODEL))],
        out_specs=row(D_MODEL),
        out_shape=jax.ShapeDtypeStruct((t, D_MODEL), F32),
        scratch_shapes=[pltpu.VMEM((2, n * TOKEN_TILE_ROWS, LANES), U32), pltpu.SemaphoreType.DMA((2,))],
        compiler_params=_params("arbitrary"),
        name="combine",
    )(slot3, slot3, x2, route, ys, final_g)


def _dispatch_tables(route, counts, t):
    tm = MOE_ROWS
    ids = route[:, 0:2].astype(I32)
    rank = route[:, 4:6].astype(I32)
    counts = counts[0, 0:N_EXPERTS].astype(I32)
    padded = (counts + tm - 1) // tm * tm
    pad_end = jnp.cumsum(padded)
    pad_start = pad_end - padded
    experts = jnp.arange(N_EXPERTS, dtype=I32)
    slot = rank + jnp.sum(jnp.where(ids[..., None] == experts, pad_start, 0), axis=-1)
    nblk = 2 * t // tm + N_EXPERTS
    blk_row = jnp.arange(nblk, dtype=I32) * tm
    blk_valid = blk_row < pad_end[-1]
    be = jnp.minimum(jnp.sum((blk_row[:, None] >= pad_end[None, :]).astype(I32), axis=1), N_EXPERTS - 1)
    onehot = (be[:, None] == experts[None, :]).astype(I32)
    pick = lambda table: jnp.sum(onehot * table[None, :], axis=1)
    blk_e = jnp.where(blk_valid, be, -1)
    blk_off = blk_row - pick(pad_start)
    blk_first = (blk_valid & (blk_off == 0)).astype(I32)
    blk_cnt = jnp.where(blk_valid, jnp.clip(pick(counts) - blk_off, 0, tm), 0)
    used = counts > 0
    order = jnp.cumsum(used.astype(I32)) - 1
    cand = jnp.where(used[None, :] & (experts[None, :] > experts[:, None]), experts[None, :], N_EXPERTS)
    nxt = jnp.min(cand, axis=1)
    nxt = jnp.where(nxt >= N_EXPERTS, -1, nxt)
    nxt2 = jnp.where(nxt >= 0, jnp.sum(jnp.where(experts[None, :] == nxt[:, None], nxt[None, :], 0), axis=1), -1)
    nvalid = pad_end[-1:] // tm
    tables = (blk_e, blk_first, pick(order) % 2, pick(nxt), pick(nxt2), blk_cnt, nvalid)
    return slot, tuple(tb.astype(I32) for tb in tables)


def _moe(x2, hp, route, counts, w13, w2, layer, final_g, final_norm):
    t = x2.shape[0]
    slot, tables = _dispatch_tables(route, counts, t)
    td = min(DISPATCH_ROWS, t)
    nblk = tables[0].shape[0]
    xs_init = jnp.zeros((nblk * MOE_ROWS * TOKEN_TILE_ROWS, LANES), U32)
    xs = _dispatch(slot.reshape(t // td, 1, 2 * td), hp, xs_init)
    ys = _experts(tables, xs, w13, w2, layer)
    tc = min(COMBINE_ROWS, t)
    slot3 = slot.reshape(t // tc, tc, 2).transpose(0, 2, 1).reshape(t // tc, 1, 2 * tc)
    return _combine(slot3, x2, route, ys, final_g, final_norm)


def _rope_tables(positions):
    inv_freq = ROPE_THETA ** (-jnp.arange(0, HEAD_DIM, 2, dtype=F32) / HEAD_DIM)
    ang = positions.astype(F32)[..., None] * inv_freq
    c, s = jnp.cos(ang), jnp.sin(ang)
    t = positions.size
    cos128 = jnp.concatenate([c, c, c, c], axis=-1).reshape(t, LANES)
    sin128 = jnp.concatenate([-s, s, -s, s], axis=-1).reshape(t, LANES)
    return cos128, sin128


def _router_weights(wg, we, bg, be):
    w = jnp.zeros((D_MODEL, ROUTE_LANES), F32).at[:, 0:N_GROUPS].set(wg)
    w = w.at[:, N_GROUPS:N_GROUPS + N_EXPERTS].set(we)
    head = w.astype(BF16)
    tail = (w - head.astype(F32)).astype(BF16)
    bias = jnp.zeros((1, ROUTE_LANES), F32).at[0, 0:N_GROUPS].set(bg)
    bias = bias.at[0, N_GROUPS:N_GROUPS + N_EXPERTS].set(be)
    return jnp.concatenate([head, tail], axis=1), bias


def kernel(x, positions, attn_norm_g, w_in, conv_dw_w, conv_dw_b, conv_ln_g, conv_ln_b, conv_pw_w, conv_out_g, swa_sinks, swa_out_g, diff_lambda_q1, diff_lambda_k1, diff_lambda_q2, diff_lambda_k2, diff_subln_g, w_out, ffn_norm_g, router_group_w, router_group_b, router_expert_w, router_expert_b, moe_w13, moe_w2, final_norm_g):
    b, s, d = x.shape
    t = b * s
    depth = w_in.shape[0]
    cos128, sin128 = _rope_tables(positions)
    x2 = x.reshape(t, d)
    row = lambda v: v.reshape(1, -1)
    for i in range(depth):
        lambda_init = 0.8 - 0.6 * float(np.exp(-0.3 * i))
        z, qs, ks, vs, qd, kd, vd = _inproj(x2, row(attn_norm_g[i]), w_in[i].astype(BF16), cos128, sin128)
        conv_o = _conv(z.reshape(b, s, CONV_CH), conv_dw_w[i], row(conv_dw_b[i]), row(conv_ln_g[i]),
                       row(conv_ln_b[i]), conv_pw_w[i].astype(BF16), row(conv_out_g[i]))
        swa_o = _swa(qs.reshape(b, s, -1), ks.reshape(b, s, -1), vs.reshape(b, s, -1),
                     row(swa_sinks[i]), row(swa_out_g[i]))
        diff_o = _diff(qd.reshape(b, s, -1), kd.reshape(b, s, -1), vd.reshape(b, s, -1),
                       row(diff_lambda_q1[i]), row(diff_lambda_k1[i]), row(diff_lambda_q2[i]),
                       row(diff_lambda_k2[i]), row(diff_subln_g[i]), lambda_init)
        wr, br = _router_weights(router_group_w[i], router_expert_w[i], router_group_b[i], router_expert_b[i])
        x2, hp, route, counts = _outproj(x2, conv_o.reshape(t, -1), swa_o.reshape(t, -1),
                                         diff_o.reshape(t, -1), w_out[i].astype(BF16), row(ffn_norm_g[i]),
                                         wr, br)
        x2 = _moe(x2, hp, route, counts, moe_w13, moe_w2, i, row(final_norm_g), i == depth - 1)
    return x2.reshape(b, s, d)
```

```python
import functools

import numpy as np
import jax
import jax.numpy as jnp
from jax import lax
from jax.experimental import pallas as pl
from jax.experimental.pallas import tpu as pltpu

F32 = jnp.float32
BF16 = jnp.bfloat16
I32 = jnp.int32
U32 = jnp.uint32

D_MODEL = 2048
HEAD_DIM = 64
CONV_CH = 512
CONV_WIDTH = 31
SWA_Q_HEADS = 12
SWA_KV_HEADS = 4
SWA_GROUP = SWA_Q_HEADS // SWA_KV_HEADS
SWA_WINDOW = 128
SWA_WIDTH = SWA_Q_HEADS * HEAD_DIM
SWA_KV_WIDTH = SWA_KV_HEADS * HEAD_DIM
DIFF_HEADS = 6
DIFF_WIDTH = DIFF_HEADS * 2 * HEAD_DIM
IN_WIDTH = 2 * CONV_CH + SWA_WIDTH + 2 * SWA_KV_WIDTH + 3 * DIFF_WIDTH
ROPE_THETA = 10000.0
N_GROUPS = 8
EXPERTS_PER_GROUP = 8
N_EXPERTS = N_GROUPS * EXPERTS_PER_GROUP
EXPERT_FF = 512
RMS_EPS = 1e-6
LN_EPS = 1e-5
NEG_INF = -1e30
LOG2_E = 1.4426950408889634

LANES = 128
SUBLANES = 8
VMEM_LIMIT_BYTES = 56 * 1024 * 1024

INPROJ_ROWS = 512
CONV_ROWS = 256
CONV_HALO = 32
DIFF_ROWS = 512
OUTPROJ_ROWS = 512
OUTPROJ_SPLIT = 1
MOE_ROWS = 256
DISPATCH_ROWS = 512
COMBINE_ROWS = 512
TOKEN_TILE_ROWS = D_MODEL // 2 // LANES
DMA_UNROLL = 8
W13_CHUNKS = 4
W2_CHUNKS = 2
CAST_ROWS = 64
ROUTE_LANES = 128


def _params(*sem):
    return pltpu.CompilerParams(dimension_semantics=sem, vmem_limit_bytes=VMEM_LIMIT_BYTES)


def _resident(shape):
    nd = len(shape)
    return pl.BlockSpec(shape, lambda *_: (0,) * nd, pipeline_mode=pl.Buffered(1))


def _rope(r, cos, sin_signed, first_half):
    outs = []
    for k in range(r.shape[1] // LANES):
        seg = r[:, k * LANES:(k + 1) * LANES]
        partner = jnp.where(first_half,
                            pltpu.roll(seg, LANES - HEAD_DIM // 2, 1),
                            pltpu.roll(seg, HEAD_DIM // 2, 1))
        outs.append(seg * cos + partner * sin_signed)
    return outs[0] if len(outs) == 1 else jnp.concatenate(outs, axis=1)


def _dup_heads(r, low_head):
    outs = []
    for k in range(r.shape[1] // LANES):
        seg = r[:, k * LANES:(k + 1) * LANES]
        rot = pltpu.roll(seg, HEAD_DIM, 1)
        outs.append(jnp.where(low_head, seg, rot))
        outs.append(jnp.where(low_head, rot, seg))
    return jnp.concatenate(outs, axis=1)


def _inproj_kernel(x_ref, g_ref, w_ref, cos_ref, sin_ref,
                   z_ref, qs_ref, ks_ref, vs_ref, qd_ref, kd_ref, vd_ref):
    x = x_ref[...]
    ms = jnp.mean(x * x, axis=-1, keepdims=True)
    h = (x * lax.rsqrt(ms + RMS_EPS) * g_ref[...]).astype(BF16)
    cos = cos_ref[...]
    sin_signed = sin_ref[...]
    lane = lax.broadcasted_iota(I32, cos.shape, 1)
    first_half = (lane & (HEAD_DIM // 2)) == 0
    nc = 2 * LANES
    q_scale = HEAD_DIM ** -0.5

    def proj(c0):
        return jnp.dot(h, w_ref[:, c0:c0 + nc], preferred_element_type=F32)

    for j in range(CONV_CH // nc):
        a = proj(j * nc)
        gate = proj(CONV_CH + j * nc)
        z_ref[:, j * nc:(j + 1) * nc] = a * jax.nn.sigmoid(gate)
    base = 2 * CONV_CH
    for j in range(SWA_WIDTH // nc):
        r = _rope(proj(base + j * nc), cos, sin_signed, first_half)
        qs_ref[:, j * nc:(j + 1) * nc] = (r * q_scale).astype(BF16)
    base += SWA_WIDTH
    low_head = lane < HEAD_DIM
    for j in range(SWA_KV_WIDTH // nc):
        r = _rope(proj(base + j * nc), cos, sin_signed, first_half)
        ks_ref[:, 2 * j * nc:2 * (j + 1) * nc] = _dup_heads(r, low_head).astype(BF16)
    base += SWA_KV_WIDTH
    for j in range(SWA_KV_WIDTH // nc):
        vs_ref[:, 2 * j * nc:2 * (j + 1) * nc] = _dup_heads(proj(base + j * nc), low_head).astype(BF16)
    base += SWA_KV_WIDTH
    for j in range(DIFF_WIDTH // nc):
        r = _rope(proj(base + j * nc), cos, sin_signed, first_half)
        qd_ref[:, j * nc:(j + 1) * nc] = (r * (q_scale * LOG2_E)).astype(BF16)
    base += DIFF_WIDTH
    for j in range(DIFF_WIDTH // nc):
        r = _rope(proj(base + j * nc), cos, sin_signed, first_half)
        kd_ref[:, j * nc:(j + 1) * nc] = r.astype(BF16)
    base += DIFF_WIDTH
    for j in range(DIFF_WIDTH // nc):
        vd_ref[:, j * nc:(j + 1) * nc] = proj(base + j * nc).astype(BF16)


def _inproj(x2, g, w_bf, cos128, sin128):
    t = x2.shape[0]
    tm = min(INPROJ_ROWS, t)
    row = lambda w: pl.BlockSpec((tm, w), lambda i: (i, 0))
    widths = (CONV_CH, SWA_WIDTH, 2 * SWA_KV_WIDTH, 2 * SWA_KV_WIDTH, DIFF_WIDTH, DIFF_WIDTH, DIFF_WIDTH)
    dtypes = (F32,) + (BF16,) * 6
    return pl.pallas_call(
        _inproj_kernel,
        grid=(t // tm,),
        in_specs=[row(D_MODEL), _resident((1, D_MODEL)), _resident((D_MODEL, IN_WIDTH)),
                  row(LANES), row(LANES)],
        out_specs=[row(w) for w in widths],
        out_shape=[jax.ShapeDtypeStruct((t, w), dt) for w, dt in zip(widths, dtypes)],
        compiler_params=_params("parallel"),
        name="inproj",
    )(x2, g, w_bf, cos128, sin128)


def _conv_kernel(z_ref, halo_ref, dww_ref, dwb_ref, lng_ref, lnb_ref, pw_ref, og_ref,
                 o_ref, buf_ref, y_ref):
    ts = z_ref.shape[1]
    i = pl.program_id(1)
    buf_ref[0, 0:CONV_HALO, :] = jnp.where(i > 0, halo_ref[0], 0.0)
    buf_ref[0, CONV_HALO:CONV_HALO + ts, :] = z_ref[0]
    kept = CONV_HALO + ts - SUBLANES
    for b in range(1, SUBLANES):
        buf_ref[b, 0:kept, :] = buf_ref[0, b:b + kept, :]
    rows = 64
    first = CONV_HALO - (CONV_WIDTH - 1)
    for c in range(CONV_CH // LANES):
        cs = slice(c * LANES, (c + 1) * LANES)
        for r in range(ts // rows):
            acc = jnp.broadcast_to(dwb_ref[:, cs], (rows, LANES))
            for j in range(CONV_WIDTH):
                b = (first + j) % SUBLANES
                s0 = first + j - b + r * rows
                acc = acc + buf_ref[b, s0:s0 + rows, cs] * dww_ref[j:j + 1, cs]
            y_ref[r * rows:(r + 1) * rows, cs] = acc
    y = y_ref[...]
    mu = jnp.mean(y, axis=-1, keepdims=True)
    yc = y - mu
    var = jnp.mean(yc * yc, axis=-1, keepdims=True)
    yn = yc * lax.rsqrt(var + LN_EPS) * lng_ref[...] + lnb_ref[...]
    act = yn * jax.nn.sigmoid(yn)
    p = jnp.dot(act.astype(BF16), pw_ref[...], preferred_element_type=F32)
    ms = jnp.mean(p * p, axis=-1, keepdims=True)
    o_ref[0] = (p * lax.rsqrt(ms + RMS_EPS) * og_ref[...]).astype(BF16)


def _conv(z3, dw_w, dw_b, ln_g, ln_b, pw_bf, out_g):
    b, s, _ = z3.shape
    ts = min(CONV_ROWS, s)
    hb = ts // CONV_HALO
    return pl.pallas_call(
        _conv_kernel,
        grid=(b, s // ts),
        in_specs=[pl.BlockSpec((1, ts, CONV_CH), lambda bi, i: (bi, i, 0)),
                  pl.BlockSpec((1, CONV_HALO, CONV_CH), lambda bi, i: (bi, jnp.maximum(i * hb - 1, 0), 0)),
                  _resident((CONV_WIDTH, CONV_CH)), _resident((1, CONV_CH)), _resident((1, CONV_CH)),
                  _resident((1, CONV_CH)), _resident((CONV_CH, CONV_CH)), _resident((1, CONV_CH))],
        out_specs=pl.BlockSpec((1, ts, CONV_CH), lambda bi, i: (bi, i, 0)),
        out_shape=jax.ShapeDtypeStruct((b, s, CONV_CH), BF16),
        scratch_shapes=[pltpu.VMEM((SUBLANES, CONV_HALO + ts, CONV_CH), F32), pltpu.VMEM((ts, CONV_CH), F32)],
        compiler_params=_params("parallel", "arbitrary"),
        name="conv",
    )(z3, z3, dw_w, dw_b, ln_g, ln_b, pw_bf, out_g)


def _swa_kernel(q_ref, kc_ref, kp_ref, vc_ref, vp_ref, sink_ref, g_ref, o_ref):
    w = SWA_WINDOW
    n = pl.program_id(1)
    rows = SWA_GROUP * w
    qi = lax.broadcasted_iota(I32, (rows, 2 * w), 0) & (w - 1)
    kj = lax.broadcasted_iota(I32, (rows, 2 * w), 1)
    valid = ((kj < w) & (kj > qi) & (n > 0)) | ((kj >= w) & (kj - w <= qi))
    low_head = lax.broadcasted_iota(I32, (w, LANES), 1) < HEAD_DIM
    contract = (((1,), (1,)), ((), ()))
    zero = jnp.zeros((w, LANES), BF16)
    heads = [None] * SWA_Q_HEADS
    for hk in range(SWA_KV_HEADS):
        cs = slice(hk * LANES, (hk + 1) * LANES)
        kk = jnp.concatenate([kp_ref[0, :, cs], kc_ref[0, :, cs]], axis=0)
        vv = jnp.concatenate([vp_ref[0, :, cs], vc_ref[0, :, cs]], axis=0)
        q_parts, sink_parts = [], []
        for gq in range(SWA_GROUP):
            hq = hk * SWA_GROUP + gq
            tile = q_ref[0, :, (hq // 2) * LANES:(hq // 2 + 1) * LANES]
            q_parts.append(jnp.where(low_head, tile, zero) if hq % 2 == 0 else jnp.where(low_head, zero, tile))
            sink_parts.append(jnp.broadcast_to(sink_ref[:, hq:hq + 1], (w, LANES)))
        q3 = jnp.concatenate(q_parts, axis=0)
        sink = jnp.concatenate(sink_parts, axis=0)
        s = lax.dot_general(q3, kk, contract, preferred_element_type=F32)
        s = jnp.where(valid, s, NEG_INF)
        m = jnp.maximum(jnp.max(s, axis=-1, keepdims=True), sink)
        e = jnp.exp(s - jnp.concatenate([m, m], axis=1))
        den = jnp.sum(e, axis=-1, keepdims=True) + jnp.exp(sink - m)
        o = jnp.dot(e.astype(BF16), vv, preferred_element_type=F32) * (1.0 / den)
        for gq in range(SWA_GROUP):
            heads[hk * SWA_GROUP + gq] = o[gq * w:(gq + 1) * w]
    tiles = []
    ss = jnp.zeros((w, 1), F32)
    for t in range(SWA_Q_HEADS // 2):
        tile = jnp.where(low_head, heads[2 * t], heads[2 * t + 1])
        ss = ss + jnp.sum(tile * tile, axis=-1, keepdims=True)
        tiles.append(tile)
    scale = lax.rsqrt(ss * (1.0 / SWA_WIDTH) + RMS_EPS)
    o_ref[0] = (jnp.concatenate(tiles, axis=1) * scale * g_ref[...]).astype(BF16)


def _swa(q3, k3, v3, sinks, out_g):
    b, s, _ = q3.shape
    w = SWA_WINDOW
    kvw = k3.shape[2]
    cur = lambda width: pl.BlockSpec((1, w, width), lambda bi, n: (bi, n, 0))
    prev = lambda width: pl.BlockSpec((1, w, width), lambda bi, n: (bi, jnp.maximum(n - 1, 0), 0))
    return pl.pallas_call(
        _swa_kernel,
        grid=(b, s // w),
        in_specs=[cur(SWA_WIDTH), cur(kvw), prev(kvw), cur(kvw), prev(kvw),
                  _resident((1, SWA_Q_HEADS)), _resident((1, SWA_WIDTH))],
        out_specs=cur(SWA_WIDTH),
        out_shape=jax.ShapeDtypeStruct((b, s, SWA_WIDTH), BF16),
        compiler_params=_params("parallel", "arbitrary"),
        name="swa",
    )(q3, k3, k3, v3, v3, sinks, out_g)


def _diff_kernel(q_ref, k_ref, v_ref, bias_ref, lq1_ref, lk1_ref, lq2_ref, lk2_ref, g_ref, o_ref,
                 q2_ref, m_ref, l_ref, acc_ref, *, lambda_init):
    tq = q_ref.shape[1]
    tk = tq
    i = pl.program_id(2)
    contract = (((1,), (1,)), ((), ()))
    q = q_ref[0]
    low_head = lax.broadcasted_iota(I32, q.shape, 1) < HEAD_DIM
    zero = jnp.zeros(q.shape, q.dtype)
    q2_ref[0:tq, :] = jnp.where(low_head, q, zero)
    q2_ref[tq:2 * tq, :] = jnp.where(low_head, zero, q)
    m_ref[...] = jnp.full(m_ref.shape, NEG_INF, F32)
    l_ref[...] = jnp.zeros(l_ref.shape, F32)
    acc_ref[...] = jnp.zeros(acc_ref.shape, F32)
    reps = tk // LANES

    def step(j, masked):
        k0 = pl.multiple_of(j * tk, tk)
        kblk = k_ref[0, pl.ds(k0, tk), :]
        vblk = v_ref[0, pl.ds(k0, tk), :]
        s = lax.dot_general(q2_ref[...], kblk, contract, preferred_element_type=F32)
        if masked:
            s = s + bias_ref[...]
        m_prev = m_ref[...]
        m_next = jnp.maximum(m_prev, jnp.max(s, axis=-1, keepdims=True))
        alpha = jnp.exp2(m_prev - m_next)
        p = jnp.exp2(s - jnp.concatenate([m_next] * reps, axis=1))
        l_ref[...] = alpha * l_ref[...] + jnp.sum(p, axis=-1, keepdims=True)
        acc_ref[...] = alpha * acc_ref[...] + jnp.dot(p.astype(BF16), vblk, preferred_element_type=F32)
        m_ref[...] = m_next

    def body(j, carry):
        step(j, False)
        return carry

    lax.fori_loop(0, i, body, 0)
    step(i, True)

    lam = (jnp.exp(jnp.sum(lq1_ref[...] * lk1_ref[...], axis=-1, keepdims=True))
           - jnp.exp(jnp.sum(lq2_ref[...] * lk2_ref[...], axis=-1, keepdims=True)) + lambda_init)
    o = acc_ref[0:tq, :] / l_ref[0:tq, :] - lam * (acc_ref[tq:2 * tq, :] / l_ref[tq:2 * tq, :])
    ms = jnp.mean(o * o, axis=-1, keepdims=True)
    o_ref[0] = (o * lax.rsqrt(ms + LN_EPS) * g_ref[...] * (1.0 - lambda_init)).astype(BF16)


def _diff(q3, k3, v3, lq1, lk1, lq2, lk2, subln_g, lambda_init):
    b, s, _ = q3.shape
    tq = min(DIFF_ROWS, s)
    hw = 2 * HEAD_DIM
    qspec = pl.BlockSpec((1, tq, hw), lambda bi, h, i: (bi, i, h))
    kvspec = pl.BlockSpec((1, s, hw), lambda bi, h, i: (bi, 0, h))
    vec = _resident((1, HEAD_DIM))
    qrow = lax.broadcasted_iota(I32, (2, tq, tq), 1).reshape(2 * tq, tq)
    kcol = lax.broadcasted_iota(I32, (2 * tq, tq), 1)
    bias = jnp.where(kcol <= qrow, 0.0, NEG_INF).astype(F32)
    return pl.pallas_call(
        functools.partial(_diff_kernel, lambda_init=lambda_init),
        grid=(b, DIFF_HEADS, s // tq),
        in_specs=[qspec, kvspec, kvspec, _resident((2 * tq, tq)), vec, vec, vec, vec, _resident((1, hw))],
        out_specs=qspec,
        out_shape=jax.ShapeDtypeStruct((b, s, DIFF_WIDTH), BF16),
        scratch_shapes=[pltpu.VMEM((2 * tq, hw), BF16), pltpu.VMEM((2 * tq, hw), F32),
                        pltpu.VMEM((2 * tq, hw), F32), pltpu.VMEM((2 * tq, hw), F32)],
        compiler_params=_params("parallel", "parallel", "arbitrary"),
        name="diff",
    )(q3, k3, v3, bias, lq1, lk1, lq2, lk2, subln_g)


def _bits(v):
    return lax.bitcast_convert_type(v, U32)


def _route(logits):
    lane_i = lax.broadcasted_iota(I32, logits.shape, 1)
    lane = lane_i.astype(F32)

    def first_lane(mask):
        return -jnp.max(jnp.where(mask, -lane, -float(ROUTE_LANES)), axis=-1, keepdims=True)

    is_group = lane_i < N_GROUPS
    gmax = jnp.max(jnp.where(is_group, logits, NEG_INF), axis=-1, keepdims=True)
    g_idx = first_lane(is_group & (logits == gmax))
    g_w = 1.0 / jnp.sum(jnp.where(is_group, jnp.exp(logits - gmax), 0.0), axis=-1, keepdims=True)
    lo = N_GROUPS + g_idx * EXPERTS_PER_GROUP
    in_sel = (lane >= lo) & (lane < lo + EXPERTS_PER_GROUP)
    e1 = jnp.max(jnp.where(in_sel, logits, NEG_INF), axis=-1, keepdims=True)
    i1 = first_lane(in_sel & (logits == e1))
    rest = in_sel & (lane != i1)
    e2 = jnp.max(jnp.where(rest, logits, NEG_INF), axis=-1, keepdims=True)
    i2 = first_lane(rest & (logits == e2))
    t = jnp.exp(e2 - e1)
    w1 = 1.0 / (1.0 + t)
    w2 = t * w1
    out = jnp.where(lane_i == 0, i1 - N_GROUPS,
          jnp.where(lane_i == 1, i2 - N_GROUPS,
          jnp.where(lane_i == 2, g_w * w1,
          jnp.where(lane_i == 3, g_w * w2, 0.0))))
    return out


def _pack_pair(lo_f32, hi_f32):
    lo = _bits(lo_f32.astype(BF16).astype(F32)) >> 16
    hi = _bits(hi_f32.astype(BF16).astype(F32)) & jnp.uint32(0xFFFF0000)
    return lo | hi


def _unpack_pair(words):
    lo = lax.bitcast_convert_type(words << 16, F32)
    hi = lax.bitcast_convert_type(words & jnp.uint32(0xFFFF0000), F32)
    return lo, hi


def _outproj_kernel(x_ref, c_ref, s_ref, d_ref, w_ref, g_ref, wr_ref, br_ref,
                    xo_ref, hp_ref, route_ref, counts_ref, run_ref):
    tm = x_ref.shape[0]

    @pl.when(pl.program_id(0) == 0)
    def _():
        run_ref[...] = jnp.zeros(run_ref.shape, F32)

    nc = 4 * LANES
    o1, o2 = CONV_CH, CONV_CH + SWA_WIDTH
    half = D_MODEL // 2
    sub = tm // OUTPROJ_SPLIT
    tri = (lax.broadcasted_iota(I32, (sub, sub), 1) < lax.broadcasted_iota(I32, (sub, sub), 0)).astype(BF16)
    for r0 in range(0, tm, sub):
        rs = slice(r0, r0 + sub)
        c_in, s_in, d_in = c_ref[rs, :], s_ref[rs, :], d_ref[rs, :]
        ss = jnp.zeros((sub, 1), F32)
        for j in range(D_MODEL // nc):
            cs = slice(j * nc, (j + 1) * nc)
            y = (jnp.dot(c_in, w_ref[0:o1, cs], preferred_element_type=F32)
                 + jnp.dot(s_in, w_ref[o1:o2, cs], preferred_element_type=F32)
                 + jnp.dot(d_in, w_ref[o2:D_MODEL, cs], preferred_element_type=F32))
            xn = x_ref[rs, cs] + y
            xo_ref[rs, cs] = xn
            ss = ss + jnp.sum(xn * xn, axis=-1, keepdims=True)
        scale = lax.rsqrt(ss * (1.0 / D_MODEL) + RMS_EPS)
        h_lo = xo_ref[rs, 0:half] * scale * g_ref[:, 0:half]
        h_hi = xo_ref[rs, half:D_MODEL] * scale * g_ref[:, half:D_MODEL]
        words = _pack_pair(h_lo, h_hi)
        for k in range(TOKEN_TILE_ROWS):
            hp_ref[pl.ds(r0 * TOKEN_TILE_ROWS + k, sub, stride=TOKEN_TILE_ROWS), :] = (
                words[:, k * LANES:(k + 1) * LANES])
        b_lo, b_hi = h_lo.astype(BF16), h_hi.astype(BF16)
        r_lo = (h_lo - b_lo.astype(F32)).astype(BF16)
        r_hi = (h_hi - b_hi.astype(F32)).astype(BF16)
        acc = (jnp.dot(b_lo, wr_ref[0:half, :], preferred_element_type=F32)
               + jnp.dot(b_hi, wr_ref[half:D_MODEL, :], preferred_element_type=F32)
               + jnp.dot(r_lo, wr_ref[0:half, :], preferred_element_type=F32)
               + jnp.dot(r_hi, wr_ref[half:D_MODEL, :], preferred_element_type=F32))
        logits = acc[:, 0:ROUTE_LANES] + acc[:, ROUTE_LANES:2 * ROUTE_LANES] + br_ref[...]
        route = _route(logits)
        lane = lax.broadcasted_iota(I32, route.shape, 1)
        lane_f = lane.astype(F32)
        hot1 = (lane_f == route[:, 0:1]).astype(F32)
        hot2 = (lane_f == route[:, 1:2]).astype(F32)
        hot = hot1 + hot2
        before = jnp.dot(tri, hot.astype(BF16), preferred_element_type=F32) + run_ref[...]
        rank1 = jnp.sum(hot1 * before, axis=-1, keepdims=True)
        rank2 = jnp.sum(hot2 * before, axis=-1, keepdims=True)
        route_ref[rs, :] = jnp.where(lane == 4, rank1, jnp.where(lane == 5, rank2, route))
        run_ref[...] = run_ref[...] + jnp.sum(hot, axis=0, keepdims=True)
    counts_ref[...] = run_ref[...]


def _outproj(x2, conv_o, swa_o, diff_o, w_bf, ffn_g, wr_bf, br):
    t = x2.shape[0]
    tm = min(OUTPROJ_ROWS, t)
    row = lambda w: pl.BlockSpec((tm, w), lambda i: (i, 0))
    return pl.pallas_call(
        _outproj_kernel,
        grid=(t // tm,),
        in_specs=[row(D_MODEL), row(CONV_CH), row(SWA_WIDTH), row(DIFF_WIDTH),
                  _resident((D_MODEL, D_MODEL)), _resident((1, D_MODEL)),
                  _resident((D_MODEL, 2 * ROUTE_LANES)), _resident((1, ROUTE_LANES))],
        out_specs=[row(D_MODEL), pl.BlockSpec((tm * TOKEN_TILE_ROWS, LANES), lambda i: (i, 0)),
                   row(ROUTE_LANES), pl.BlockSpec((1, ROUTE_LANES), lambda i: (0, 0))],
        out_shape=[jax.ShapeDtypeStruct((t, D_MODEL), F32),
                   jax.ShapeDtypeStruct((t * TOKEN_TILE_ROWS, LANES), U32),
                   jax.ShapeDtypeStruct((t, ROUTE_LANES), F32),
                   jax.ShapeDtypeStruct((1, ROUTE_LANES), F32)],
        scratch_shapes=[pltpu.VMEM((1, ROUTE_LANES), F32)],
        compiler_params=_params("arbitrary"),
        name="outproj",
    )(x2, conv_o, swa_o, diff_o, w_bf, ffn_g, wr_bf, br)


def _dispatch_kernel(slot_ref, hp_ref, init_ref, xs_ref, sem):
    del init_ref
    n = slot_ref.shape[2]
    rows = TOKEN_TILE_ROWS

    def copy(src_row, dst_row):
        return pltpu.make_async_copy(hp_ref.at[pl.ds(src_row, rows), :], xs_ref.at[pl.ds(dst_row, rows), :], sem)

    def start(t, c):
        src = pl.multiple_of(t * rows, rows)
        for k in range(2):
            dst = pl.multiple_of(slot_ref[0, 0, 2 * t + k] * rows, rows)
            copy(src, dst).start(priority=k)
        return c
    lax.fori_loop(0, n // 2, start, 0, unroll=DMA_UNROLL)

    def wait(a, c):
        copy(0, 0).wait()
        return c
    lax.fori_loop(0, n, wait, 0, unroll=2 * DMA_UNROLL)


def _dispatch(slot3, hp, xs_init):
    nt, _, n = slot3.shape
    td = n // 2
    return pl.pallas_call(
        _dispatch_kernel,
        grid=(nt,),
        in_specs=[pl.BlockSpec((1, 1, n), lambda i: (i, 0, 0), memory_space=pltpu.SMEM),
                  pl.BlockSpec((td * TOKEN_TILE_ROWS, LANES), lambda i: (i, 0)),
                  pl.BlockSpec(memory_space=pl.ANY)],
        out_specs=pl.BlockSpec(memory_space=pl.ANY),
        out_shape=jax.ShapeDtypeStruct(xs_init.shape, xs_init.dtype),
        scratch_shapes=[pltpu.SemaphoreType.DMA],
        input_output_aliases={2: 0},
        compiler_params=_params("arbitrary"),
        name="dispatch",
    )(slot3, hp, xs_init)


def _experts_kernel(be_ref, bfirst_ref, bslot_ref, bnext_ref, bnext2_ref, bcnt_ref, nvalid_ref,
                    xs_ref, w13_hbm, w2_hbm, ys_ref,
                    w13_buf, w2_buf, w13_bf, w2_bf, sem13, sem2, *, layer):
    del nvalid_ref
    i = pl.program_id(0)
    e = be_ref[i]
    valid = e >= 0
    slot = bslot_ref[i]

    def w_copies(expert, s):
        copies = []
        r13 = D_MODEL // W13_CHUNKS
        for c in range(W13_CHUNKS):
            rs = pl.ds(c * r13, r13)
            copies.append(pltpu.make_async_copy(w13_hbm.at[layer, expert, rs, :], w13_buf.at[s, rs, :],
                                                sem13.at[s]))
        r2 = EXPERT_FF // W2_CHUNKS
        for c in range(W2_CHUNKS):
            rs = pl.ds(c * r2, r2)
            copies.append(pltpu.make_async_copy(w2_hbm.at[layer, expert, rs, :], w2_buf.at[s, rs, :],
                                                sem2.at[s]))
        return copies

    def start_all(copies):
        for c, cp in enumerate(copies):
            cp.start(priority=c % 2)

    @pl.when(valid & (bfirst_ref[i] == 1))
    def _():
        nxt = bnext_ref[i]
        nxt2 = bnext2_ref[i]

        @pl.when(i == 0)
        def _():
            start_all(w_copies(e, slot))

            @pl.when(nxt >= 0)
            def _():
                start_all(w_copies(nxt, 1 - slot))
        for cp in w_copies(e, slot):
            cp.wait()
        def cast13(c, carry):
            rs = pl.ds(pl.multiple_of(c * CAST_ROWS, CAST_ROWS), CAST_ROWS)
            w13_bf[rs, :] = w13_buf[slot, rs, :].astype(BF16)
            return carry
        lax.fori_loop(0, D_MODEL // CAST_ROWS, cast13, 0)

        def cast2(c, carry):
            rs = pl.ds(pl.multiple_of(c * CAST_ROWS, CAST_ROWS), CAST_ROWS)
            w2_bf[rs, :] = w2_buf[slot, rs, :].astype(BF16)
            return carry
        lax.fori_loop(0, EXPERT_FF // CAST_ROWS, cast2, 0)

        @pl.when(nxt2 >= 0)
        def _():
            start_all(w_copies(nxt2, slot))

    @pl.when(valid)
    def _():
        half = D_MODEL // 2
        tm = xs_ref.shape[0] // TOKEN_TILE_ROWS
        live = lax.broadcasted_iota(I32, (tm, LANES), 0) < bcnt_ref[i]
        los, his = [], []
        for k in range(TOKEN_TILE_ROWS):
            words = jnp.where(live, xs_ref[pl.ds(k, tm, stride=TOKEN_TILE_ROWS), :], jnp.uint32(0))
            lo, hi = _unpack_pair(words)
            los.append(lo.astype(BF16))
            his.append(hi.astype(BF16))
        x_lo = jnp.concatenate(los, axis=1)
        x_hi = jnp.concatenate(his, axis=1)
        h = (jnp.dot(x_lo, w13_bf[0:half, :], preferred_element_type=F32)
             + jnp.dot(x_hi, w13_bf[half:D_MODEL, :], preferred_element_type=F32))
        a, b = h[:, 0:EXPERT_FF], h[:, EXPERT_FF:2 * EXPERT_FF]
        act = (a * jax.nn.sigmoid(a) * b).astype(BF16)
        y = jnp.dot(act, w2_bf[...], preferred_element_type=F32)
        for k in range(TOKEN_TILE_ROWS):
            ys_ref[pl.ds(k, tm, stride=TOKEN_TILE_ROWS), :] = _pack_pair(
                y[:, k * LANES:(k + 1) * LANES], y[:, half + k * LANES:half + (k + 1) * LANES])


def _experts(tables, xs, w13, w2, layer):
    nblk = tables[0].shape[0]
    rows = xs.shape[0] // nblk
    blk = pl.BlockSpec((rows, LANES), lambda i, *tb: (jnp.minimum(i, tb[-1][0] - 1), 0))
    return pl.pallas_call(
        functools.partial(_experts_kernel, layer=layer),
        grid_spec=pltpu.PrefetchScalarGridSpec(
            num_scalar_prefetch=len(tables),
            grid=(nblk,),
            in_specs=[blk, pl.BlockSpec(memory_space=pl.ANY), pl.BlockSpec(memory_space=pl.ANY)],
            out_specs=blk,
            scratch_shapes=[pltpu.VMEM((2, D_MODEL, 2 * EXPERT_FF), F32),
                            pltpu.VMEM((2, EXPERT_FF, D_MODEL), F32),
                            pltpu.VMEM((D_MODEL, 2 * EXPERT_FF), BF16),
                            pltpu.VMEM((EXPERT_FF, D_MODEL), BF16),
                            pltpu.SemaphoreType.DMA((2,)), pltpu.SemaphoreType.DMA((2,))],
        ),
        out_shape=jax.ShapeDtypeStruct(xs.shape, U32),
        input_output_aliases={len(tables): 0},
        compiler_params=_params("arbitrary"),
        name="experts",
    )(*tables, xs, w13, w2)


def _combine_kernel(slot_ref, slot_next_ref, x_ref, route_ref, ys_hbm, g_ref, o_ref, buf, sem, *, final_norm):
    tm = x_ref.shape[0]
    n = slot_ref.shape[2]
    rows = TOKEN_TILE_ROWS
    half = D_MODEL // 2
    i = pl.program_id(0)
    cur = i % 2

    def copy(src_row, b, dst_row):
        return pltpu.make_async_copy(ys_hbm.at[pl.ds(src_row, rows), :], buf.at[b, pl.ds(dst_row, rows), :],
                                     sem.at[b])

    def request(slots, b):
        def start(t, c):
            for k in range(2):
                a = k * tm + t
                src = pl.multiple_of(slots[0, 0, a] * rows, rows)
                copy(src, b, pl.multiple_of(a * rows, rows)).start(priority=k)
            return c
        lax.fori_loop(0, tm, start, 0, unroll=DMA_UNROLL)

    @pl.when(i == 0)
    def _():
        request(slot_ref, 0)

    @pl.when(i + 1 < pl.num_programs(0))
    def _():
        request(slot_next_ref, 1 - cur)

    def wait(a, c):
        copy(0, cur, 0).wait()
        return c
    lax.fori_loop(0, n, wait, 0, unroll=2 * DMA_UNROLL)

    g1 = route_ref[:, 2:3]
    g2 = route_ref[:, 3:4]
    ss = jnp.zeros((tm, 1), F32)
    for k in range(rows):
        lo1, hi1 = _unpack_pair(buf[cur, pl.ds(k, tm, stride=rows), :])
        lo2, hi2 = _unpack_pair(buf[cur, pl.ds(tm * rows + k, tm, stride=rows), :])
        c_lo = slice(k * LANES, (k + 1) * LANES)
        c_hi = slice(half + k * LANES, half + (k + 1) * LANES)
        x_lo = x_ref[:, c_lo] + (g1 * lo1 + g2 * lo2)
        x_hi = x_ref[:, c_hi] + (g1 * hi1 + g2 * hi2)
        o_ref[:, c_lo] = x_lo
        o_ref[:, c_hi] = x_hi
        if final_norm:
            ss = (ss + jnp.sum(x_lo * x_lo, axis=-1, keepdims=True)
                  + jnp.sum(x_hi * x_hi, axis=-1, keepdims=True))
    if final_norm:
        scale = lax.rsqrt(ss * (1.0 / D_MODEL) + RMS_EPS)
        o_ref[...] = o_ref[...] * scale * g_ref[...]


def _combine(slot3, x2, route, ys, final_g, final_norm):
    t = x2.shape[0]
    nt, _, n = slot3.shape
    tm = n // 2
    row = lambda w: pl.BlockSpec((tm, w), lambda i: (i, 0))
    return pl.pallas_call(
        functools.partial(_combine_kernel, final_norm=final_norm),
        grid=(nt,),
        in_specs=[pl.BlockSpec((1, 1, n), lambda i: (i, 0, 0), memory_space=pltpu.SMEM),
                  pl.BlockSpec((1, 1, n), lambda i: (jnp.minimum(i + 1, nt - 1), 0, 0),
                               memory_space=pltpu.SMEM),
                  row(D_MODEL), row(ROUTE_LANES), pl.BlockSpec(memory_space=pl.ANY),
                  _resident((1, D_MODEL))],
        out_specs=row(D_MODEL),
        out_shape=jax.ShapeDtypeStruct((t, D_MODEL), F32),
        scratch_shapes=[pltpu.VMEM((2, n * TOKEN_TILE_ROWS, LANES), U32), pltpu.SemaphoreType.DMA((2,))],
        compiler_params=_params("arbitrary"),
        name="combine",
    )(slot3, slot3, x2, route, ys, final_g)


def _dispatch_tables(route, counts, t):
    tm = MOE_ROWS
    ids = route[:, 0:2].astype(I32)
    rank = route[:, 4:6].astype(I32)
    counts = counts[0, 0:N_EXPERTS].astype(I32)
    padded = (counts + tm - 1) // tm * tm
    pad_end = jnp.cumsum(padded)
    pad_start = pad_end - padded
    experts = jnp.arange(N_EXPERTS, dtype=I32)
    slot = rank + jnp.sum(jnp.where(ids[..., None] == experts, pad_start, 0), axis=-1)
    nblk = 2 * t // tm + N_EXPERTS
    blk_row = jnp.arange(nblk, dtype=I32) * tm
    blk_valid = blk_row < pad_end[-1]
    be = jnp.minimum(jnp.sum((blk_row[:, None] >= pad_end[None, :]).astype(I32), axis=1), N_EXPERTS - 1)
    onehot = (be[:, None] == experts[None, :]).astype(I32)
    pick = lambda table: jnp.sum(onehot * table[None, :], axis=1)
    blk_e = jnp.where(blk_valid, be, -1)
    blk_off = blk_row - pick(pad_start)
    blk_first = (blk_valid & (blk_off == 0)).astype(I32)
    blk_cnt = jnp.where(blk_valid, jnp.clip(pick(counts) - blk_off, 0, tm), 0)
    used = counts > 0
    order = jnp.cumsum(used.astype(I32)) - 1
    cand = jnp.where(used[None, :] & (experts[None, :] > experts[:, None]), experts[None, :], N_EXPERTS)
    nxt = jnp.min(cand, axis=1)
    nxt = jnp.where(nxt >= N_EXPERTS, -1, nxt)
    nxt2 = jnp.where(nxt >= 0, jnp.sum(jnp.where(experts[None, :] == nxt[:, None], nxt[None, :], 0), axis=1), -1)
    nvalid = pad_end[-1:] // tm
    tables = (blk_e, blk_first, pick(order) % 2, pick(nxt), pick(nxt2), blk_cnt, nvalid)
    return slot, tuple(tb.astype(I32) for tb in tables)


def _moe(x2, hp, route, counts, w13, w2, layer, final_g, final_norm):
    t = x2.shape[0]
    slot, tables = _dispatch_tables(route, counts, t)
    td = min(DISPATCH_ROWS, t)
    nblk = tables[0].shape[0]
    xs_init = jnp.zeros((nblk * MOE_ROWS * TOKEN_TILE_ROWS, LANES), U32)
    xs = _dispatch(slot.reshape(t // td, 1, 2 * td), hp, xs_init)
    ys = _experts(tables, xs, w13, w2, layer)
    tc = min(COMBINE_ROWS, t)
    slot3 = slot.reshape(t // tc, tc, 2).transpose(0, 2, 1).reshape(t // tc, 1, 2 * tc)
    return _combine(slot3, x2, route, ys, final_g, final_norm)


def _rope_tables(positions):
    inv_freq = ROPE_THETA ** (-jnp.arange(0, HEAD_DIM, 2, dtype=F32) / HEAD_DIM)
    ang = positions.astype(F32)[..., None] * inv_freq
    c, s = jnp.cos(ang), jnp.sin(ang)
    t = positions.size
    cos128 = jnp.concatenate([c, c, c, c], axis=-1).reshape(t, LANES)
    sin128 = jnp.concatenate([-s, s, -s, s], axis=-1).reshape(t, LANES)
    return cos128, sin128


def _router_weights(wg, we, bg, be):
    w = jnp.zeros((D_MODEL, ROUTE_LANES), F32).at[:, 0:N_GROUPS].set(wg)
    w = w.at[:, N_GROUPS:N_GROUPS + N_EXPERTS].set(we)
    head = w.astype(BF16)
    tail = (w - head.astype(F32)).astype(BF16)
    bias = jnp.zeros((1, ROUTE_LANES), F32).at[0, 0:N_GROUPS].set(bg)
    bias = bias.at[0, N_GROUPS:N_GROUPS + N_EXPERTS].set(be)
    return jnp.concatenate([head, tail], axis=1), bias


def kernel(x, positions, attn_norm_g, w_in, conv_dw_w, conv_dw_b, conv_ln_g, conv_ln_b, conv_pw_w, conv_out_g, swa_sinks, swa_out_g, diff_lambda_q1, diff_lambda_k1, diff_lambda_q2, diff_lambda_k2, diff_subln_g, w_out, ffn_norm_g, router_group_w, router_group_b, router_expert_w, router_expert_b, moe_w13, moe_w2, final_norm_g):
    b, s, d = x.shape
    t = b * s
    depth = w_in.shape[0]
    cos128, sin128 = _rope_tables(positions)
    x2 = x.reshape(t, d)
    row = lambda v: v.reshape(1, -1)
    for i in range(depth):
        lambda_init = 0.8 - 0.6 * float(np.exp(-0.3 * i))
        z, qs, ks, vs, qd, kd, vd = _inproj(x2, row(attn_norm_g[i]), w_in[i].astype(BF16), cos128, sin128)
        conv_o = _conv(z.reshape(b, s, CONV_CH), conv_dw_w[i], row(conv_dw_b[i]), row(conv_ln_g[i]),
                       row(conv_ln_b[i]), conv_pw_w[i].astype(BF16), row(conv_out_g[i]))
        swa_o = _swa(qs.reshape(b, s, -1), ks.reshape(b, s, -1), vs.reshape(b, s, -1),
                     row(swa_sinks[i]), row(swa_out_g[i]))
        diff_o = _diff(qd.reshape(b, s, -1), kd.reshape(b, s, -1), vd.reshape(b, s, -1),
                       row(diff_lambda_q1[i]), row(diff_lambda_k1[i]), row(diff_lambda_q2[i]),
                       row(diff_lambda_k2[i]), row(diff_subln_g[i]), lambda_init)
        wr, br = _router_weights(router_group_w[i], router_expert_w[i], router_group_b[i], router_expert_b[i])
        x2, hp, route, counts = _outproj(x2, conv_o.reshape(t, -1), swa_o.reshape(t, -1),
                                         diff_o.reshape(t, -1), w_out[i].astype(BF16), row(ffn_norm_g[i]),
                                         wr, br)
        x2 = _moe(x2, hp, route, counts, moe_w13, moe_w2, i, row(final_norm_g), i == depth - 1)
    return x2.reshape(b, s, d)
```

```python
import functools

import numpy as np
import jax
import jax.numpy as jnp
from jax import lax
from jax.experimental import pallas as pl
from jax.experimental.pallas import tpu as pltpu

F32 = jnp.float32
BF16 = jnp.bfloat16
I32 = jnp.int32
U32 = jnp.uint32

D_MODEL = 2048
HEAD_DIM = 64
CONV_CH = 512
CONV_WIDTH = 31
SWA_Q_HEADS = 12
SWA_KV_HEADS = 4
SWA_GROUP = SWA_Q_HEADS // SWA_KV_HEADS
SWA_WINDOW = 128
SWA_WIDTH = SWA_Q_HEADS * HEAD_DIM
SWA_KV_WIDTH = SWA_KV_HEADS * HEAD_DIM
DIFF_HEADS = 6
DIFF_WIDTH = DIFF_HEADS * 2 * HEAD_DIM
IN_WIDTH = 2 * CONV_CH + SWA_WIDTH + 2 * SWA_KV_WIDTH + 3 * DIFF_WIDTH
ROPE_THETA = 10000.0
N_GROUPS = 8
EXPERTS_PER_GROUP = 8
N_EXPERTS = N_GROUPS * EXPERTS_PER_GROUP
EXPERT_FF = 512
RMS_EPS = 1e-6
LN_EPS = 1e-5
NEG_INF = -1e30
LOG2_E = 1.4426950408889634

LANES = 128
SUBLANES = 8
VMEM_LIMIT_BYTES = 56 * 1024 * 1024

INPROJ_ROWS = 512
CONV_ROWS = 256
CONV_HALO = 32
DIFF_ROWS = 512
OUTPROJ_ROWS = 512
OUTPROJ_SPLIT = 2
MOE_ROWS = 256
DISPATCH_ROWS = 512
COMBINE_ROWS = 256
TOKEN_TILE_ROWS = D_MODEL // 2 // LANES
DMA_UNROLL = 8
W13_CHUNKS = 4
W2_CHUNKS = 2
CAST_ROWS = 64
ROUTE_LANES = 128


def _params(*sem):
    return pltpu.CompilerParams(dimension_semantics=sem, vmem_limit_bytes=VMEM_LIMIT_BYTES)


def _resident(shape):
    nd = len(shape)
    return pl.BlockSpec(shape, lambda *_: (0,) * nd, pipeline_mode=pl.Buffered(1))


def _rope(r, cos, sin_signed, first_half):
    outs = []
    for k in range(r.shape[1] // LANES):
        seg = r[:, k * LANES:(k + 1) * LANES]
        partner = jnp.where(first_half,
                            pltpu.roll(seg, LANES - HEAD_DIM // 2, 1),
                            pltpu.roll(seg, HEAD_DIM // 2, 1))
        outs.append(seg * cos + partner * sin_signed)
    return outs[0] if len(outs) == 1 else jnp.concatenate(outs, axis=1)


def _dup_heads(r, low_head):
    outs = []
    for k in range(r.shape[1] // LANES):
        seg = r[:, k * LANES:(k + 1) * LANES]
        rot = pltpu.roll(seg, HEAD_DIM, 1)
        outs.append(jnp.where(low_head, seg, rot))
        outs.append(jnp.where(low_head, rot, seg))
    return jnp.concatenate(outs, axis=1)


def _inproj_kernel(x_ref, g_ref, w_ref, cos_ref, sin_ref,
                   z_ref, qs_ref, ks_ref, vs_ref, qd_ref, kd_ref, vd_ref):
    x = x_ref[...]
    ms = jnp.mean(x * x, axis=-1, keepdims=True)
    h = (x * lax.rsqrt(ms + RMS_EPS) * g_ref[...]).astype(BF16)
    cos = cos_ref[...]
    sin_signed = sin_ref[...]
    lane = lax.broadcasted_iota(I32, cos.shape, 1)
    first_half = (lane & (HEAD_DIM // 2)) == 0
    nc = 2 * LANES
    q_scale = HEAD_DIM ** -0.5

    def proj(c0):
        return jnp.dot(h, w_ref[:, c0:c0 + nc], preferred_element_type=F32)

    for j in range(CONV_CH // nc):
        a = proj(j * nc)
        gate = proj(CONV_CH + j * nc)
        z_ref[:, j * nc:(j + 1) * nc] = a * jax.nn.sigmoid(gate)
    base = 2 * CONV_CH
    for j in range(SWA_WIDTH // nc):
        r = _rope(proj(base + j * nc), cos, sin_signed, first_half)
        qs_ref[:, j * nc:(j + 1) * nc] = (r * q_scale).astype(BF16)
    base += SWA_WIDTH
    low_head = lane < HEAD_DIM
    for j in range(SWA_KV_WIDTH // nc):
        r = _rope(proj(base + j * nc), cos, sin_signed, first_half)
        ks_ref[:, 2 * j * nc:2 * (j + 1) * nc] = _dup_heads(r, low_head).astype(BF16)
    base += SWA_KV_WIDTH
    for j in range(SWA_KV_WIDTH // nc):
        vs_ref[:, 2 * j * nc:2 * (j + 1) * nc] = _dup_heads(proj(base + j * nc), low_head).astype(BF16)
    base += SWA_KV_WIDTH
    for j in range(DIFF_WIDTH // nc):
        r = _rope(proj(base + j * nc), cos, sin_signed, first_half)
        qd_ref[:, j * nc:(j + 1) * nc] = (r * (q_scale * LOG2_E)).astype(BF16)
    base += DIFF_WIDTH
    for j in range(DIFF_WIDTH // nc):
        r = _rope(proj(base + j * nc), cos, sin_signed, first_half)
        kd_ref[:, j * nc:(j + 1) * nc] = r.astype(BF16)
    base += DIFF_WIDTH
    for j in range(DIFF_WIDTH // nc):
        vd_ref[:, j * nc:(j + 1) * nc] = proj(base + j * nc).astype(BF16)


def _inproj(x2, g, w_bf, cos128, sin128):
    t = x2.shape[0]
    tm = min(INPROJ_ROWS, t)
    row = lambda w: pl.BlockSpec((tm, w), lambda i: (i, 0))
    widths = (CONV_CH, SWA_WIDTH, 2 * SWA_KV_WIDTH, 2 * SWA_KV_WIDTH, DIFF_WIDTH, DIFF_WIDTH, DIFF_WIDTH)
    dtypes = (F32,) + (BF16,) * 6
    return pl.pallas_call(
        _inproj_kernel,
        grid=(t // tm,),
        in_specs=[row(D_MODEL), _resident((1, D_MODEL)), _resident((D_MODEL, IN_WIDTH)),
                  row(LANES), row(LANES)],
        out_specs=[row(w) for w in widths],
        out_shape=[jax.ShapeDtypeStruct((t, w), dt) for w, dt in zip(widths, dtypes)],
        compiler_params=_params("parallel"),
        name="inproj",
    )(x2, g, w_bf, cos128, sin128)


def _conv_kernel(z_ref, halo_ref, dww_ref, dwb_ref, lng_ref, lnb_ref, pw_ref, og_ref,
                 o_ref, buf_ref, y_ref):
    ts = z_ref.shape[1]
    i = pl.program_id(1)
    buf_ref[0, 0:CONV_HALO, :] = jnp.where(i > 0, halo_ref[0], 0.0)
    buf_ref[0, CONV_HALO:CONV_HALO + ts, :] = z_ref[0]
    kept = CONV_HALO + ts - SUBLANES
    for b in range(1, SUBLANES):
        buf_ref[b, 0:kept, :] = buf_ref[0, b:b + kept, :]
    rows = 64
    first = CONV_HALO - (CONV_WIDTH - 1)
    for c in range(CONV_CH // LANES):
        cs = slice(c * LANES, (c + 1) * LANES)
        for r in range(ts // rows):
            acc = jnp.broadcast_to(dwb_ref[:, cs], (rows, LANES))
            for j in range(CONV_WIDTH):
                b = (first + j) % SUBLANES
                s0 = first + j - b + r * rows
                acc = acc + buf_ref[b, s0:s0 + rows, cs] * dww_ref[j:j + 1, cs]
            y_ref[r * rows:(r + 1) * rows, cs] = acc
    y = y_ref[...]
    mu = jnp.mean(y, axis=-1, keepdims=True)
    yc = y - mu
    var = jnp.mean(yc * yc, axis=-1, keepdims=True)
    yn = yc * lax.rsqrt(var + LN_EPS) * lng_ref[...] + lnb_ref[...]
    act = yn * jax.nn.sigmoid(yn)
    p = jnp.dot(act.astype(BF16), pw_ref[...], preferred_element_type=F32)
    ms = jnp.mean(p * p, axis=-1, keepdims=True)
    o_ref[0] = (p * lax.rsqrt(ms + RMS_EPS) * og_ref[...]).astype(BF16)


def _conv(z3, dw_w, dw_b, ln_g, ln_b, pw_bf, out_g):
    b, s, _ = z3.shape
    ts = min(CONV_ROWS, s)
    hb = ts // CONV_HALO
    return pl.pallas_call(
        _conv_kernel,
        grid=(b, s // ts),
        in_specs=[pl.BlockSpec((1, ts, CONV_CH), lambda bi, i: (bi, i, 0)),
                  pl.BlockSpec((1, CONV_HALO, CONV_CH), lambda bi, i: (bi, jnp.maximum(i * hb - 1, 0), 0)),
                  _resident((CONV_WIDTH, CONV_CH)), _resident((1, CONV_CH)), _resident((1, CONV_CH)),
                  _resident((1, CONV_CH)), _resident((CONV_CH, CONV_CH)), _resident((1, CONV_CH))],
        out_specs=pl.BlockSpec((1, ts, CONV_CH), lambda bi, i: (bi, i, 0)),
        out_shape=jax.ShapeDtypeStruct((b, s, CONV_CH), BF16),
        scratch_shapes=[pltpu.VMEM((SUBLANES, CONV_HALO + ts, CONV_CH), F32), pltpu.VMEM((ts, CONV_CH), F32)],
        compiler_params=_params("parallel", "arbitrary"),
        name="conv",
    )(z3, z3, dw_w, dw_b, ln_g, ln_b, pw_bf, out_g)


def _swa_kernel(q_ref, kc_ref, kp_ref, vc_ref, vp_ref, sink_ref, g_ref, o_ref):
    w = SWA_WINDOW
    n = pl.program_id(1)
    rows = SWA_GROUP * w
    qi = lax.broadcasted_iota(I32, (rows, 2 * w), 0) & (w - 1)
    kj = lax.broadcasted_iota(I32, (rows, 2 * w), 1)
    valid = ((kj < w) & (kj > qi) & (n > 0)) | ((kj >= w) & (kj - w <= qi))
    low_head = lax.broadcasted_iota(I32, (w, LANES), 1) < HEAD_DIM
    contract = (((1,), (1,)), ((), ()))
    zero = jnp.zeros((w, LANES), BF16)
    heads = [None] * SWA_Q_HEADS
    for hk in range(SWA_KV_HEADS):
        cs = slice(hk * LANES, (hk + 1) * LANES)
        kk = jnp.concatenate([kp_ref[0, :, cs], kc_ref[0, :, cs]], axis=0)
        vv = jnp.concatenate([vp_ref[0, :, cs], vc_ref[0, :, cs]], axis=0)
        q_parts, sink_parts = [], []
        for gq in range(SWA_GROUP):
            hq = hk * SWA_GROUP + gq
            tile = q_ref[0, :, (hq // 2) * LANES:(hq // 2 + 1) * LANES]
            q_parts.append(jnp.where(low_head, tile, zero) if hq % 2 == 0 else jnp.where(low_head, zero, tile))
            sink_parts.append(jnp.broadcast_to(sink_ref[:, hq:hq + 1], (w, LANES)))
        q3 = jnp.concatenate(q_parts, axis=0)
        sink = jnp.concatenate(sink_parts, axis=0)
        s = lax.dot_general(q3, kk, contract, preferred_element_type=F32)
        s = jnp.where(valid, s, NEG_INF)
        m = jnp.maximum(jnp.max(s, axis=-1, keepdims=True), sink)
        e = jnp.exp(s - jnp.concatenate([m, m], axis=1))
        den = jnp.sum(e, axis=-1, keepdims=True) + jnp.exp(sink - m)
        o = jnp.dot(e.astype(BF16), vv, preferred_element_type=F32) * (1.0 / den)
        for gq in range(SWA_GROUP):
            heads[hk * SWA_GROUP + gq] = o[gq * w:(gq + 1) * w]
    tiles = []
    ss = jnp.zeros((w, 1), F32)
    for t in range(SWA_Q_HEADS // 2):
        tile = jnp.where(low_head, heads[2 * t], heads[2 * t + 1])
        ss = ss + jnp.sum(tile * tile, axis=-1, keepdims=True)
        tiles.append(tile)
    scale = lax.rsqrt(ss * (1.0 / SWA_WIDTH) + RMS_EPS)
    o_ref[0] = (jnp.concatenate(tiles, axis=1) * scale * g_ref[...]).astype(BF16)


def _swa(q3, k3, v3, sinks, out_g):
    b, s, _ = q3.shape
    w = SWA_WINDOW
    kvw = k3.shape[2]
    cur = lambda width: pl.BlockSpec((1, w, width), lambda bi, n: (bi, n, 0))
    prev = lambda width: pl.BlockSpec((1, w, width), lambda bi, n: (bi, jnp.maximum(n - 1, 0), 0))
    return pl.pallas_call(
        _swa_kernel,
        grid=(b, s // w),
        in_specs=[cur(SWA_WIDTH), cur(kvw), prev(kvw), cur(kvw), prev(kvw),
                  _resident((1, SWA_Q_HEADS)), _resident((1, SWA_WIDTH))],
        out_specs=cur(SWA_WIDTH),
        out_shape=jax.ShapeDtypeStruct((b, s, SWA_WIDTH), BF16),
        compiler_params=_params("parallel", "arbitrary"),
        name="swa",
    )(q3, k3, k3, v3, v3, sinks, out_g)


def _diff_kernel(q_ref, k_ref, v_ref, bias_ref, lq1_ref, lk1_ref, lq2_ref, lk2_ref, g_ref, o_ref,
                 q2_ref, m_ref, l_ref, acc_ref, *, lambda_init):
    tq = q_ref.shape[1]
    tk = tq
    i = pl.program_id(2)
    contract = (((1,), (1,)), ((), ()))
    q = q_ref[0]
    low_head = lax.broadcasted_iota(I32, q.shape, 1) < HEAD_DIM
    zero = jnp.zeros(q.shape, q.dtype)
    q2_ref[0:tq, :] = jnp.where(low_head, q, zero)
    q2_ref[tq:2 * tq, :] = jnp.where(low_head, zero, q)
    m_ref[...] = jnp.full(m_ref.shape, NEG_INF, F32)
    l_ref[...] = jnp.zeros(l_ref.shape, F32)
    acc_ref[...] = jnp.zeros(acc_ref.shape, F32)
    reps = tk // LANES

    def step(j, masked):
        k0 = pl.multiple_of(j * tk, tk)
        kblk = k_ref[0, pl.ds(k0, tk), :]
        vblk = v_ref[0, pl.ds(k0, tk), :]
        s = lax.dot_general(q2_ref[...], kblk, contract, preferred_element_type=F32)
        if masked:
            s = s + bias_ref[...]
        m_prev = m_ref[...]
        m_next = jnp.maximum(m_prev, jnp.max(s, axis=-1, keepdims=True))
        alpha = jnp.exp2(m_prev - m_next)
        p = jnp.exp2(s - jnp.concatenate([m_next] * reps, axis=1))
        l_ref[...] = alpha * l_ref[...] + jnp.sum(p, axis=-1, keepdims=True)
        acc_ref[...] = alpha * acc_ref[...] + jnp.dot(p.astype(BF16), vblk, preferred_element_type=F32)
        m_ref[...] = m_next

    def body(j, carry):
        step(j, False)
        return carry

    lax.fori_loop(0, i, body, 0)
    step(i, True)

    lam = (jnp.exp(jnp.sum(lq1_ref[...] * lk1_ref[...], axis=-1, keepdims=True))
           - jnp.exp(jnp.sum(lq2_ref[...] * lk2_ref[...], axis=-1, keepdims=True)) + lambda_init)
    o = acc_ref[0:tq, :] / l_ref[0:tq, :] - lam * (acc_ref[tq:2 * tq, :] / l_ref[tq:2 * tq, :])
    ms = jnp.mean(o * o, axis=-1, keepdims=True)
    o_ref[0] = (o * lax.rsqrt(ms + LN_EPS) * g_ref[...] * (1.0 - lambda_init)).astype(BF16)


def _diff(q3, k3, v3, lq1, lk1, lq2, lk2, subln_g, lambda_init):
    b, s, _ = q3.shape
    tq = min(DIFF_ROWS, s)
    hw = 2 * HEAD_DIM
    qspec = pl.BlockSpec((1, tq, hw), lambda bi, h, i: (bi, i, h))
    kvspec = pl.BlockSpec((1, s, hw), lambda bi, h, i: (bi, 0, h))
    vec = _resident((1, HEAD_DIM))
    qrow = lax.broadcasted_iota(I32, (2, tq, tq), 1).reshape(2 * tq, tq)
    kcol = lax.broadcasted_iota(I32, (2 * tq, tq), 1)
    bias = jnp.where(kcol <= qrow, 0.0, NEG_INF).astype(F32)
    return pl.pallas_call(
        functools.partial(_diff_kernel, lambda_init=lambda_init),
        grid=(b, DIFF_HEADS, s // tq),
        in_specs=[qspec, kvspec, kvspec, _resident((2 * tq, tq)), vec, vec, vec, vec, _resident((1, hw))],
        out_specs=qspec,
        out_shape=jax.ShapeDtypeStruct((b, s, DIFF_WIDTH), BF16),
        scratch_shapes=[pltpu.VMEM((2 * tq, hw), BF16), pltpu.VMEM((2 * tq, hw), F32),
                        pltpu.VMEM((2 * tq, hw), F32), pltpu.VMEM((2 * tq, hw), F32)],
        compiler_params=_params("parallel", "parallel", "arbitrary"),
        name="diff",
    )(q3, k3, v3, bias, lq1, lk1, lq2, lk2, subln_g)


def _bits(v):
    return lax.bitcast_convert_type(v, U32)


def _route(logits):
    lane_i = lax.broadcasted_iota(I32, logits.shape, 1)
    lane = lane_i.astype(F32)

    def first_lane(mask):
        return -jnp.max(jnp.where(mask, -lane, -float(ROUTE_LANES)), axis=-1, keepdims=True)

    is_group = lane_i < N_GROUPS
    gmax = jnp.max(jnp.where(is_group, logits, NEG_INF), axis=-1, keepdims=True)
    g_idx = first_lane(is_group & (logits == gmax))
    g_w = 1.0 / jnp.sum(jnp.where(is_group, jnp.exp(logits - gmax), 0.0), axis=-1, keepdims=True)
    lo = N_GROUPS + g_idx * EXPERTS_PER_GROUP
    in_sel = (lane >= lo) & (lane < lo + EXPERTS_PER_GROUP)
    e1 = jnp.max(jnp.where(in_sel, logits, NEG_INF), axis=-1, keepdims=True)
    i1 = first_lane(in_sel & (logits == e1))
    rest = in_sel & (lane != i1)
    e2 = jnp.max(jnp.where(rest, logits, NEG_INF), axis=-1, keepdims=True)
    i2 = first_lane(rest & (logits == e2))
    t = jnp.exp(e2 - e1)
    w1 = 1.0 / (1.0 + t)
    w2 = t * w1
    out = jnp.where(lane_i == 0, i1 - N_GROUPS,
          jnp.where(lane_i == 1, i2 - N_GROUPS,
          jnp.where(lane_i == 2, g_w * w1,
          jnp.where(lane_i == 3, g_w * w2, 0.0))))
    return out


def _pack_pair(lo_f32, hi_f32):
    lo = _bits(lo_f32.astype(BF16).astype(F32)) >> 16
    hi = _bits(hi_f32.astype(BF16).astype(F32)) & jnp.uint32(0xFFFF0000)
    return lo | hi


def _unpack_pair(words):
    lo = lax.bitcast_convert_type(words << 16, F32)
    hi = lax.bitcast_convert_type(words & jnp.uint32(0xFFFF0000), F32)
    return lo, hi


def _outproj_kernel(x_ref, c_ref, s_ref, d_ref, w_ref, g_ref, wr_ref, br_ref,
                    xo_ref, hp_ref, route_ref, counts_ref, run_ref):
    tm = x_ref.shape[0]

    @pl.when(pl.program_id(0) == 0)
    def _():
        run_ref[...] = jnp.zeros(run_ref.shape, F32)

    nc = 4 * LANES
    n_chunks = D_MODEL // nc
    half = D_MODEL // 2
    sub = tm // OUTPROJ_SPLIT
    tri = (lax.broadcasted_iota(I32, (sub, sub), 1) < lax.broadcasted_iota(I32, (sub, sub), 0)).astype(BF16)

    def main_chunk(rs, mix, j, ss, acc):
        cs = slice(j * nc, (j + 1) * nc)
        xn = x_ref[rs, cs] + jnp.dot(mix, w_ref[:, cs], preferred_element_type=F32)
        xo_ref[rs, cs] = xn
        ss = ss + jnp.sum(xn * xn, axis=-1, keepdims=True)
        head = xn.astype(BF16)
        tail = (xn - head.astype(F32)).astype(BF16)
        wrj = wr_ref[j * nc:(j + 1) * nc, :]
        acc = (acc + jnp.dot(head, wrj, preferred_element_type=F32)
               + jnp.dot(tail, wrj, preferred_element_type=F32))
        return ss, acc

    def make_epilogue(r0, ss, acc):
        rs = slice(r0, r0 + sub)
        scale = lax.rsqrt(ss * (1.0 / D_MODEL) + RMS_EPS)
        wpc = half // n_chunks

        def piece(j):
            lo = slice(j * wpc, (j + 1) * wpc)
            hi = slice(half + j * wpc, half + (j + 1) * wpc)
            words = _pack_pair(xo_ref[rs, lo] * scale * g_ref[:, lo], xo_ref[rs, hi] * scale * g_ref[:, hi])
            for kk in range(wpc // LANES):
                k = j * (wpc // LANES) + kk
                hp_ref[pl.ds(r0 * TOKEN_TILE_ROWS + k, sub, stride=TOKEN_TILE_ROWS), :] = (
                    words[:, kk * LANES:(kk + 1) * LANES])
            if j < n_chunks - 1:
                return
            logits = (acc[:, 0:ROUTE_LANES] + acc[:, ROUTE_LANES:2 * ROUTE_LANES]) * scale + br_ref[...]
            route = _route(logits)
            lane = lax.broadcasted_iota(I32, route.shape, 1)
            lane_f = lane.astype(F32)
            hot1 = (lane_f == route[:, 0:1]).astype(F32)
            hot2 = (lane_f == route[:, 1:2]).astype(F32)
            hot = hot1 + hot2
            before = jnp.dot(tri, hot.astype(BF16), preferred_element_type=F32) + run_ref[...]
            rank1 = jnp.sum(hot1 * before, axis=-1, keepdims=True)
            rank2 = jnp.sum(hot2 * before, axis=-1, keepdims=True)
            route_ref[rs, :] = jnp.where(lane == 4, rank1, jnp.where(lane == 5, rank2, route))
            run_ref[...] = run_ref[...] + jnp.sum(hot, axis=0, keepdims=True)
        return piece

    pending = None
    for r0 in range(0, tm, sub):
        rs = slice(r0, r0 + sub)
        mix = jnp.concatenate([c_ref[rs, :], s_ref[rs, :], d_ref[rs, :]], axis=1)
        ss = jnp.zeros((sub, 1), F32)
        acc = jnp.zeros((sub, 2 * ROUTE_LANES), F32)
        for j in range(n_chunks):
            ss, acc = main_chunk(rs, mix, j, ss, acc)
            if pending is not None:
                pending(j)
        pending = make_epilogue(r0, ss, acc)
    for j in range(n_chunks):
        pending(j)
    counts_ref[...] = run_ref[...]


def _outproj(x2, conv_o, swa_o, diff_o, w_bf, ffn_g, wr_bf, br):
    t = x2.shape[0]
    tm = min(OUTPROJ_ROWS, t)
    row = lambda w: pl.BlockSpec((tm, w), lambda i: (i, 0))
    return pl.pallas_call(
        _outproj_kernel,
        grid=(t // tm,),
        in_specs=[row(D_MODEL), row(CONV_CH), row(SWA_WIDTH), row(DIFF_WIDTH),
                  _resident((D_MODEL, D_MODEL)), _resident((1, D_MODEL)),
                  _resident((D_MODEL, 2 * ROUTE_LANES)), _resident((1, ROUTE_LANES))],
        out_specs=[row(D_MODEL), pl.BlockSpec((tm * TOKEN_TILE_ROWS, LANES), lambda i: (i, 0)),
                   row(ROUTE_LANES), pl.BlockSpec((1, ROUTE_LANES), lambda i: (0, 0))],
        out_shape=[jax.ShapeDtypeStruct((t, D_MODEL), F32),
                   jax.ShapeDtypeStruct((t * TOKEN_TILE_ROWS, LANES), U32),
                   jax.ShapeDtypeStruct((t, ROUTE_LANES), F32),
                   jax.ShapeDtypeStruct((1, ROUTE_LANES), F32)],
        scratch_shapes=[pltpu.VMEM((1, ROUTE_LANES), F32)],
        compiler_params=_params("arbitrary"),
        name="outproj",
    )(x2, conv_o, swa_o, diff_o, w_bf, ffn_g, wr_bf, br)


def _dispatch_kernel(slot_ref, hp_ref, init_ref, xs_ref, sem):
    del init_ref
    n = slot_ref.shape[2]
    rows = TOKEN_TILE_ROWS

    def copy(src_row, dst_row):
        return pltpu.make_async_copy(hp_ref.at[pl.ds(src_row, rows), :], xs_ref.at[pl.ds(dst_row, rows), :], sem)

    def start(t, c):
        src = pl.multiple_of(t * rows, rows)
        for k in range(2):
            dst = pl.multiple_of(slot_ref[0, 0, 2 * t + k] * rows, rows)
            copy(src, dst).start(priority=k)
        return c
    lax.fori_loop(0, n // 2, start, 0, unroll=DMA_UNROLL)

    def wait(a, c):
        copy(0, 0).wait()
        return c
    lax.fori_loop(0, n, wait, 0, unroll=2 * DMA_UNROLL)


def _dispatch(slot3, hp, xs_init):
    nt, _, n = slot3.shape
    td = n // 2
    return pl.pallas_call(
        _dispatch_kernel,
        grid=(nt,),
        in_specs=[pl.BlockSpec((1, 1, n), lambda i: (i, 0, 0), memory_space=pltpu.SMEM),
                  pl.BlockSpec((td * TOKEN_TILE_ROWS, LANES), lambda i: (i, 0)),
                  pl.BlockSpec(memory_space=pl.ANY)],
        out_specs=pl.BlockSpec(memory_space=pl.ANY),
        out_shape=jax.ShapeDtypeStruct(xs_init.shape, xs_init.dtype),
        scratch_shapes=[pltpu.SemaphoreType.DMA],
        input_output_aliases={2: 0},
        compiler_params=_params("arbitrary"),
        name="dispatch",
    )(slot3, hp, xs_init)


def _experts_kernel(be_ref, bfirst_ref, bslot_ref, bnext_ref, bnext2_ref, bcnt_ref, nvalid_ref,
                    xs_ref, w13_hbm, w2_hbm, ys_ref,
                    w13_buf, w2_buf, w13_bf, w2_bf, sem13, sem2, *, layer):
    del nvalid_ref
    i = pl.program_id(0)
    e = be_ref[i]
    valid = e >= 0
    slot = bslot_ref[i]

    def w_copies(expert, s):
        copies = []
        r13 = D_MODEL // W13_CHUNKS
        for c in range(W13_CHUNKS):
            rs = pl.ds(c * r13, r13)
            copies.append(pltpu.make_async_copy(w13_hbm.at[layer, expert, rs, :], w13_buf.at[s, rs, :],
                                                sem13.at[s]))
        r2 = EXPERT_FF // W2_CHUNKS
        for c in range(W2_CHUNKS):
            rs = pl.ds(c * r2, r2)
            copies.append(pltpu.make_async_copy(w2_hbm.at[layer, expert, rs, :], w2_buf.at[s, rs, :],
                                                sem2.at[s]))
        return copies

    def start_all(copies):
        for c, cp in enumerate(copies):
            cp.start(priority=c % 2)

    @pl.when(valid & (bfirst_ref[i] == 1))
    def _():
        nxt = bnext_ref[i]
        nxt2 = bnext2_ref[i]

        @pl.when(i == 0)
        def _():
            start_all(w_copies(e, slot))

            @pl.when(nxt >= 0)
            def _():
                start_all(w_copies(nxt, 1 - slot))
        for cp in w_copies(e, slot):
            cp.wait()
        def cast13(c, carry):
            rs = pl.ds(pl.multiple_of(c * CAST_ROWS, CAST_ROWS), CAST_ROWS)
            w13_bf[rs, :] = w13_buf[slot, rs, :].astype(BF16)
            return carry
        lax.fori_loop(0, D_MODEL // CAST_ROWS, cast13, 0)

        def cast2(c, carry):
            rs = pl.ds(pl.multiple_of(c * CAST_ROWS, CAST_ROWS), CAST_ROWS)
            w2_bf[rs, :] = w2_buf[slot, rs, :].astype(BF16)
            return carry
        lax.fori_loop(0, EXPERT_FF // CAST_ROWS, cast2, 0)

        @pl.when(nxt2 >= 0)
        def _():
            start_all(w_copies(nxt2, slot))

    @pl.when(valid)
    def _():
        half = D_MODEL // 2
        tm = xs_ref.shape[0] // TOKEN_TILE_ROWS
        live = lax.broadcasted_iota(I32, (tm, LANES), 0) < bcnt_ref[i]
        los, his = [], []
        for k in range(TOKEN_TILE_ROWS):
            words = jnp.where(live, xs_ref[pl.ds(k, tm, stride=TOKEN_TILE_ROWS), :], jnp.uint32(0))
            lo, hi = _unpack_pair(words)
            los.append(lo.astype(BF16))
            his.append(hi.astype(BF16))
        x_lo = jnp.concatenate(los, axis=1)
        x_hi = jnp.concatenate(his, axis=1)
        h = (jnp.dot(x_lo, w13_bf[0:half, :], preferred_element_type=F32)
             + jnp.dot(x_hi, w13_bf[half:D_MODEL, :], preferred_element_type=F32))
        a, b = h[:, 0:EXPERT_FF], h[:, EXPERT_FF:2 * EXPERT_FF]
        act = (a * jax.nn.sigmoid(a) * b).astype(BF16)
        y = jnp.dot(act, w2_bf[...], preferred_element_type=F32)
        for k in range(TOKEN_TILE_ROWS):
            ys_ref[pl.ds(k, tm, stride=TOKEN_TILE_ROWS), :] = _pack_pair(
                y[:, k * LANES:(k + 1) * LANES], y[:, half + k * LANES:half + (k + 1) * LANES])


def _experts(tables, xs, w13, w2, layer):
    nblk = tables[0].shape[0]
    rows = xs.shape[0] // nblk
    blk = pl.BlockSpec((rows, LANES), lambda i, *tb: (jnp.minimum(i, tb[-1][0] - 1), 0))
    return pl.pallas_call(
        functools.partial(_experts_kernel, layer=layer),
        grid_spec=pltpu.PrefetchScalarGridSpec(
            num_scalar_prefetch=len(tables),
            grid=(nblk,),
            in_specs=[blk, pl.BlockSpec(memory_space=pl.ANY), pl.BlockSpec(memory_space=pl.ANY)],
            out_specs=blk,
            scratch_shapes=[pltpu.VMEM((2, D_MODEL, 2 * EXPERT_FF), F32),
                            pltpu.VMEM((2, EXPERT_FF, D_MODEL), F32),
                            pltpu.VMEM((D_MODEL, 2 * EXPERT_FF), BF16),
                            pltpu.VMEM((EXPERT_FF, D_MODEL), BF16),
                            pltpu.SemaphoreType.DMA((2,)), pltpu.SemaphoreType.DMA((2,))],
        ),
        out_shape=jax.ShapeDtypeStruct(xs.shape, U32),
        input_output_aliases={len(tables): 0},
        compiler_params=_params("arbitrary"),
        name="experts",
    )(*tables, xs, w13, w2)


def _combine_kernel(slot_ref, slot_next_ref, x_ref, route_ref, ys_hbm, g_ref, o_ref, buf, sem, *, final_norm):
    tm = x_ref.shape[0]
    n = slot_ref.shape[2]
    rows = TOKEN_TILE_ROWS
    half = D_MODEL // 2
    i = pl.program_id(0)
    cur = i % 2

    def copy(src_row, b, dst_row):
        return pltpu.make_async_copy(ys_hbm.at[pl.ds(src_row, rows), :], buf.at[b, pl.ds(dst_row, rows), :],
                                     sem.at[b])

    def request(slots, b):
        def start(t, c):
            for k in range(2):
                a = k * tm + t
                src = pl.multiple_of(slots[0, 0, a] * rows, rows)
                copy(src, b, pl.multiple_of(a * rows, rows)).start(priority=k)
            return c
        lax.fori_loop(0, tm, start, 0, unroll=DMA_UNROLL)

    @pl.when(i == 0)
    def _():
        request(slot_ref, 0)

    @pl.when(i + 1 < pl.num_programs(0))
    def _():
        request(slot_next_ref, 1 - cur)

    def wait(a, c):
        copy(0, cur, 0).wait()
        return c
    lax.fori_loop(0, n, wait, 0, unroll=2 * DMA_UNROLL)

    g1 = route_ref[:, 2:3]
    g2 = route_ref[:, 3:4]
    ss = jnp.zeros((tm, 1), F32)
    for k in range(rows):
        lo1, hi1 = _unpack_pair(buf[cur, pl.ds(k, tm, stride=rows), :])
        lo2, hi2 = _unpack_pair(buf[cur, pl.ds(tm * rows + k, tm, stride=rows), :])
        c_lo = slice(k * LANES, (k + 1) * LANES)
        c_hi = slice(half + k * LANES, half + (k + 1) * LANES)
        x_lo = x_ref[:, c_lo] + (g1 * lo1 + g2 * lo2)
        x_hi = x_ref[:, c_hi] + (g1 * hi1 + g2 * hi2)
        o_ref[:, c_lo] = x_lo
        o_ref[:, c_hi] = x_hi
        if final_norm:
            ss = (ss + jnp.sum(x_lo * x_lo, axis=-1, keepdims=True)
                  + jnp.sum(x_hi * x_hi, axis=-1, keepdims=True))
    if final_norm:
        scale = lax.rsqrt(ss * (1.0 / D_MODEL) + RMS_EPS)
        o_ref[...] = o_ref[...] * scale * g_ref[...]


def _combine(slot3, x2, route, ys, final_g, final_norm):
    t = x2.shape[0]
    nt, _, n = slot3.shape
    tm = n // 2
    row = lambda w: pl.BlockSpec((tm, w), lambda i: (i, 0))
    return pl.pallas_call(
        functools.partial(_combine_kernel, final_norm=final_norm),
        grid=(nt,),
        in_specs=[pl.BlockSpec((1, 1, n), lambda i: (i, 0, 0), memory_space=pltpu.SMEM),
                  pl.BlockSpec((1, 1, n), lambda i: (jnp.minimum(i + 1, nt - 1), 0, 0),
                               memory_space=pltpu.SMEM),
                  row(D_MODEL), row(ROUTE_LANES), pl.BlockSpec(memory_space=pl.ANY),
                  _resident((1, D_MODEL))],
        out_specs=row(D_MODEL),
        out_shape=jax.ShapeDtypeStruct((t, D_MODEL), F32),
        scratch_shapes=[pltpu.VMEM((2, n * TOKEN_TILE_ROWS, LANES), U32), pltpu.SemaphoreType.DMA((2,))],
        compiler_params=_params("arbitrary"),
        name="combine",
    )(slot3, slot3, x2, route, ys, final_g)


def _dispatch_tables(route, counts, t):
    tm = MOE_ROWS
    ids = route[:, 0:2].astype(I32)
    rank = route[:, 4:6].astype(I32)
    counts = counts[0, 0:N_EXPERTS].astype(I32)
    padded = (counts + tm - 1) // tm * tm
    pad_end = jnp.cumsum(padded)
    pad_start = pad_end - padded
    experts = jnp.arange(N_EXPERTS, dtype=I32)
    slot = rank + jnp.sum(jnp.where(ids[..., None] == experts, pad_start, 0), axis=-1)
    nblk = 2 * t // tm + N_EXPERTS
    blk_row = jnp.arange(nblk, dtype=I32) * tm
    blk_valid = blk_row < pad_end[-1]
    be = jnp.minimum(jnp.sum((blk_row[:, None] >= pad_end[None, :]).astype(I32), axis=1), N_EXPERTS - 1)
    onehot = (be[:, None] == experts[None, :]).astype(I32)
    pick = lambda table: jnp.sum(onehot * table[None, :], axis=1)
    blk_e = jnp.where(blk_valid, be, -1)
    blk_off = blk_row - pick(pad_start)
    blk_first = (blk_valid & (blk_off == 0)).astype(I32)
    blk_cnt = jnp.where(blk_valid, jnp.clip(pick(counts) - blk_off, 0, tm), 0)
    used = counts > 0
    order = jnp.cumsum(used.astype(I32)) - 1
    cand = jnp.where(used[None, :] & (experts[None, :] > experts[:, None]), experts[None, :], N_EXPERTS)
    nxt = jnp.min(cand, axis=1)
    nxt = jnp.where(nxt >= N_EXPERTS, -1, nxt)
    nxt2 = jnp.where(nxt >= 0, jnp.sum(jnp.where(experts[None, :] == nxt[:, None], nxt[None, :], 0), axis=1), -1)
    nvalid = pad_end[-1:] // tm
    tables = (blk_e, blk_first, pick(order) % 2, pick(nxt), pick(nxt2), blk_cnt, nvalid)
    return slot, tuple(tb.astype(I32) for tb in tables)


def _moe(x2, hp, route, counts, w13, w2, layer, final_g, final_norm):
    t = x2.shape[0]
    slot, tables = _dispatch_tables(route, counts, t)
    td = min(DISPATCH_ROWS, t)
    nblk = tables[0].shape[0]
    xs_init = jnp.zeros((nblk * MOE_ROWS * TOKEN_TILE_ROWS, LANES), U32)
    xs = _dispatch(slot.reshape(t // td, 1, 2 * td), hp, xs_init)
    ys = _experts(tables, xs, w13, w2, layer)
    tc = min(COMBINE_ROWS, t)
    slot3 = slot.reshape(t // tc, tc, 2).transpose(0, 2, 1).reshape(t // tc, 1, 2 * tc)
    return _combine(slot3, x2, route, ys, final_g, final_norm)


def _rope_tables(positions):
    inv_freq = ROPE_THETA ** (-jnp.arange(0, HEAD_DIM, 2, dtype=F32) / HEAD_DIM)
    ang = positions.astype(F32)[..., None] * inv_freq
    c, s = jnp.cos(ang), jnp.sin(ang)
    t = positions.size
    cos128 = jnp.concatenate([c, c, c, c], axis=-1).reshape(t, LANES)
    sin128 = jnp.concatenate([-s, s, -s, s], axis=-1).reshape(t, LANES)
    return cos128, sin128


def _router_weights(wg, we, bg, be, norm_g):
    w = jnp.zeros((D_MODEL, ROUTE_LANES), F32).at[:, 0:N_GROUPS].set(wg)
    w = w.at[:, N_GROUPS:N_GROUPS + N_EXPERTS].set(we)
    w = w * norm_g[:, None]
    head = w.astype(BF16)
    tail = (w - head.astype(F32)).astype(BF16)
    bias = jnp.zeros((1, ROUTE_LANES), F32).at[0, 0:N_GROUPS].set(bg)
    bias = bias.at[0, N_GROUPS:N_GROUPS + N_EXPERTS].set(be)
    return jnp.concatenate([head, tail], axis=1), bias


def kernel(x, positions, attn_norm_g, w_in, conv_dw_w, conv_dw_b, conv_ln_g, conv_ln_b, conv_pw_w, conv_out_g, swa_sinks, swa_out_g, diff_lambda_q1, diff_lambda_k1, diff_lambda_q2, diff_lambda_k2, diff_subln_g, w_out, ffn_norm_g, router_group_w, router_group_b, router_expert_w, router_expert_b, moe_w13, moe_w2, final_norm_g):
    b, s, d = x.shape
    t = b * s
    depth = w_in.shape[0]
    cos128, sin128 = _rope_tables(positions)
    x2 = x.reshape(t, d)
    row = lambda v: v.reshape(1, -1)
    for i in range(depth):
        lambda_init = 0.8 - 0.6 * float(np.exp(-0.3 * i))
        z, qs, ks, vs, qd, kd, vd = _inproj(x2, row(attn_norm_g[i]), w_in[i].astype(BF16), cos128, sin128)
        conv_o = _conv(z.reshape(b, s, CONV_CH), conv_dw_w[i], row(conv_dw_b[i]), row(conv_ln_g[i]),
                       row(conv_ln_b[i]), conv_pw_w[i].astype(BF16), row(conv_out_g[i]))
        swa_o = _swa(qs.reshape(b, s, -1), ks.reshape(b, s, -1), vs.reshape(b, s, -1),
                     row(swa_sinks[i]), row(swa_out_g[i]))
        diff_o = _diff(qd.reshape(b, s, -1), kd.reshape(b, s, -1), vd.reshape(b, s, -1),
                       row(diff_lambda_q1[i]), row(diff_lambda_k1[i]), row(diff_lambda_q2[i]),
                       row(diff_lambda_k2[i]), row(diff_subln_g[i]), lambda_init)
        wr, br = _router_weights(router_group_w[i], router_expert_w[i], router_group_b[i], router_expert_b[i],
                                 ffn_norm_g[i])
        x2, hp, route, counts = _outproj(x2, conv_o.reshape(t, -1), swa_o.reshape(t, -1),
                                         diff_o.reshape(t, -1), w_out[i].astype(BF16), row(ffn_norm_g[i]),
                                         wr, br)
        x2 = _moe(x2, hp, route, counts, moe_w13, moe_w2, i, row(final_norm_g), i == depth - 1)
    return x2.reshape(b, s, d)
```

```python
import functools

import numpy as np
import jax
import jax.numpy as jnp
from jax import lax
from jax.experimental import pallas as pl
from jax.experimental.pallas import tpu as pltpu

F32 = jnp.float32
BF16 = jnp.bfloat16
I32 = jnp.int32
U32 = jnp.uint32

D_MODEL = 2048
HEAD_DIM = 64
CONV_CH = 512
CONV_WIDTH = 31
SWA_Q_HEADS = 12
SWA_KV_HEADS = 4
SWA_GROUP = SWA_Q_HEADS // SWA_KV_HEADS
SWA_WINDOW = 128
SWA_WIDTH = SWA_Q_HEADS * HEAD_DIM
SWA_KV_WIDTH = SWA_KV_HEADS * HEAD_DIM
DIFF_HEADS = 6
DIFF_WIDTH = DIFF_HEADS * 2 * HEAD_DIM
IN_WIDTH = 2 * CONV_CH + SWA_WIDTH + 2 * SWA_KV_WIDTH + 3 * DIFF_WIDTH
ROPE_THETA = 10000.0
N_GROUPS = 8
EXPERTS_PER_GROUP = 8
N_EXPERTS = N_GROUPS * EXPERTS_PER_GROUP
EXPERT_FF = 512
RMS_EPS = 1e-6
LN_EPS = 1e-5
NEG_INF = -1e30
LOG2_E = 1.4426950408889634

LANES = 128
SUBLANES = 8
VMEM_LIMIT_BYTES = 56 * 1024 * 1024

INPROJ_ROWS = 512
CONV_ROWS = 1024
CONV_HALO = 32
DIFF_ROWS = 512
OUTPROJ_ROWS = 512
OUTPROJ_SPLIT = 2
MOE_ROWS = 256
DISPATCH_ROWS = 1024
COMBINE_ROWS = 256
TOKEN_TILE_ROWS = D_MODEL // 2 // LANES
DMA_UNROLL = 8
W13_CHUNKS = 4
W2_CHUNKS = 2
CAST_ROWS = 256
ROUTE_LANES = 128


def _params(*sem):
    return pltpu.CompilerParams(dimension_semantics=sem, vmem_limit_bytes=VMEM_LIMIT_BYTES)


def _resident(shape):
    nd = len(shape)
    return pl.BlockSpec(shape, lambda *_: (0,) * nd, pipeline_mode=pl.Buffered(1))


def _resident_layer(shape, layer):
    nd = len(shape)
    return pl.BlockSpec((None,) + tuple(shape), lambda *_: (layer,) + (0,) * nd, pipeline_mode=pl.Buffered(1))


def _rope(r, cos, sin_signed, first_half):
    outs = []
    for k in range(r.shape[1] // LANES):
        seg = r[:, k * LANES:(k + 1) * LANES]
        partner = jnp.where(first_half,
                            pltpu.roll(seg, LANES - HEAD_DIM // 2, 1),
                            pltpu.roll(seg, HEAD_DIM // 2, 1))
        outs.append(seg * cos + partner * sin_signed)
    return outs[0] if len(outs) == 1 else jnp.concatenate(outs, axis=1)


def _dup_heads(r, low_head):
    outs = []
    for k in range(r.shape[1] // LANES):
        seg = r[:, k * LANES:(k + 1) * LANES]
        rot = pltpu.roll(seg, HEAD_DIM, 1)
        outs.append(jnp.where(low_head, seg, rot))
        outs.append(jnp.where(low_head, rot, seg))
    return jnp.concatenate(outs, axis=1)


def _inproj_kernel(x_ref, g_ref, w_ref, cos_ref, sin_ref,
                   z_ref, qs_ref, ks_ref, vs_ref, qd_ref, kd_ref, vd_ref):
    x = x_ref[...]
    ms = jnp.mean(x * x, axis=-1, keepdims=True)
    h = (x * lax.rsqrt(ms + RMS_EPS) * g_ref[...]).astype(BF16)
    cos = cos_ref[...]
    sin_signed = sin_ref[...]
    lane = lax.broadcasted_iota(I32, cos.shape, 1)
    first_half = (lane & (HEAD_DIM // 2)) == 0
    nc = 2 * LANES
    q_scale = HEAD_DIM ** -0.5

    def proj(c0):
        return jnp.dot(h, w_ref[:, c0:c0 + nc], preferred_element_type=F32)

    for j in range(CONV_CH // nc):
        a = proj(j * nc)
        gate = proj(CONV_CH + j * nc)
        z_ref[:, j * nc:(j + 1) * nc] = a * jax.nn.sigmoid(gate)
    base = 2 * CONV_CH
    for j in range(SWA_WIDTH // nc):
        r = _rope(proj(base + j * nc), cos, sin_signed, first_half)
        qs_ref[:, j * nc:(j + 1) * nc] = (r * (q_scale * LOG2_E)).astype(BF16)
    base += SWA_WIDTH
    low_head = lane < HEAD_DIM
    for j in range(SWA_KV_WIDTH // nc):
        r = _rope(proj(base + j * nc), cos, sin_signed, first_half)
        ks_ref[:, 2 * j * nc:2 * (j + 1) * nc] = _dup_heads(r, low_head).astype(BF16)
    base += SWA_KV_WIDTH
    for j in range(SWA_KV_WIDTH // nc):
        vs_ref[:, 2 * j * nc:2 * (j + 1) * nc] = _dup_heads(proj(base + j * nc), low_head).astype(BF16)
    base += SWA_KV_WIDTH
    for j in range(DIFF_WIDTH // nc):
        r = _rope(proj(base + j * nc), cos, sin_signed, first_half)
        qd_ref[:, j * nc:(j + 1) * nc] = (r * (q_scale * LOG2_E)).astype(BF16)
    base += DIFF_WIDTH
    for j in range(DIFF_WIDTH // nc):
        r = _rope(proj(base + j * nc), cos, sin_signed, first_half)
        kd_ref[:, j * nc:(j + 1) * nc] = r.astype(BF16)
    base += DIFF_WIDTH
    for j in range(DIFF_WIDTH // nc):
        vd_ref[:, j * nc:(j + 1) * nc] = proj(base + j * nc).astype(BF16)


def _inproj(x2, g, w_stack, layer, cos128, sin128):
    t = x2.shape[0]
    tm = min(INPROJ_ROWS, t)
    row = lambda w: pl.BlockSpec((tm, w), lambda i: (i, 0))
    widths = (CONV_CH, SWA_WIDTH, 2 * SWA_KV_WIDTH, 2 * SWA_KV_WIDTH, DIFF_WIDTH, DIFF_WIDTH, DIFF_WIDTH)
    dtypes = (F32,) + (BF16,) * 6
    return pl.pallas_call(
        _inproj_kernel,
        grid=(t // tm,),
        in_specs=[row(D_MODEL), _resident((1, D_MODEL)), _resident_layer((D_MODEL, IN_WIDTH), layer),
                  row(LANES), row(LANES)],
        out_specs=[row(w) for w in widths],
        out_shape=[jax.ShapeDtypeStruct((t, w), dt) for w, dt in zip(widths, dtypes)],
        compiler_params=_params("parallel"),
        name="inproj",
    )(x2, g, w_stack, cos128, sin128)


def _conv_kernel(z_ref, halo_ref, dww_ref, dwb_ref, lng_ref, lnb_ref, pw_ref, og_ref,
                 o_ref, buf_ref, y_ref):
    ts = z_ref.shape[1]
    i = pl.program_id(1)
    buf_ref[0, 0:CONV_HALO, :] = jnp.where(i > 0, halo_ref[0], 0.0)
    buf_ref[0, CONV_HALO:CONV_HALO + ts, :] = z_ref[0]
    kept = CONV_HALO + ts - SUBLANES
    for b in range(1, SUBLANES):
        buf_ref[b, 0:kept, :] = buf_ref[0, b:b + kept, :]
    rows = 64
    first = CONV_HALO - (CONV_WIDTH - 1)
    for c in range(CONV_CH // LANES):
        cs = slice(c * LANES, (c + 1) * LANES)
        for r in range(ts // rows):
            acc = jnp.broadcast_to(dwb_ref[:, cs], (rows, LANES))
            for j in range(CONV_WIDTH):
                b = (first + j) % SUBLANES
                s0 = first + j - b + r * rows
                acc = acc + buf_ref[b, s0:s0 + rows, cs] * dww_ref[j:j + 1, cs]
            y_ref[r * rows:(r + 1) * rows, cs] = acc
    y = y_ref[...]
    mu = jnp.mean(y, axis=-1, keepdims=True)
    yc = y - mu
    var = jnp.mean(yc * yc, axis=-1, keepdims=True)
    yn = yc * lax.rsqrt(var + LN_EPS) * lng_ref[...] + lnb_ref[...]
    act = yn * jax.nn.sigmoid(yn)
    p = jnp.dot(act.astype(BF16), pw_ref[...], preferred_element_type=F32)
    ms = jnp.mean(p * p, axis=-1, keepdims=True)
    o_ref[0] = (p * lax.rsqrt(ms + RMS_EPS) * og_ref[...]).astype(BF16)


def _conv(z3, dw_w, dw_b, ln_g, ln_b, pw_bf, out_g):
    b, s, _ = z3.shape
    ts = min(CONV_ROWS, s)
    hb = ts // CONV_HALO
    return pl.pallas_call(
        _conv_kernel,
        grid=(b, s // ts),
        in_specs=[pl.BlockSpec((1, ts, CONV_CH), lambda bi, i: (bi, i, 0)),
                  pl.BlockSpec((1, CONV_HALO, CONV_CH), lambda bi, i: (bi, jnp.maximum(i * hb - 1, 0), 0)),
                  _resident((CONV_WIDTH, CONV_CH)), _resident((1, CONV_CH)), _resident((1, CONV_CH)),
                  _resident((1, CONV_CH)), _resident((CONV_CH, CONV_CH)), _resident((1, CONV_CH))],
        out_specs=pl.BlockSpec((1, ts, CONV_CH), lambda bi, i: (bi, i, 0)),
        out_shape=jax.ShapeDtypeStruct((b, s, CONV_CH), BF16),
        scratch_shapes=[pltpu.VMEM((SUBLANES, CONV_HALO + ts, CONV_CH), F32), pltpu.VMEM((ts, CONV_CH), F32)],
        compiler_params=_params("parallel", "arbitrary"),
        name="conv",
    )(z3, z3, dw_w, dw_b, ln_g, ln_b, pw_bf, out_g)


def _swa_kernel(q_ref, kc_ref, kp_ref, vc_ref, vp_ref, sink_ref, g_ref, o_ref):
    w = SWA_WINDOW
    n = pl.program_id(1)
    rows = SWA_GROUP * w
    qi = lax.broadcasted_iota(I32, (rows, 2 * w), 0) & (w - 1)
    kj = lax.broadcasted_iota(I32, (rows, 2 * w), 1)
    valid = ((kj < w) & (kj > qi) & (n > 0)) | ((kj >= w) & (kj - w <= qi))
    low_head = lax.broadcasted_iota(I32, (w, LANES), 1) < HEAD_DIM
    contract = (((1,), (1,)), ((), ()))
    zero = jnp.zeros((w, LANES), BF16)
    heads = [None] * SWA_Q_HEADS
    for hk in range(SWA_KV_HEADS):
        cs = slice(hk * LANES, (hk + 1) * LANES)
        kk = jnp.concatenate([kp_ref[0, :, cs], kc_ref[0, :, cs]], axis=0)
        vv = jnp.concatenate([vp_ref[0, :, cs], vc_ref[0, :, cs]], axis=0)
        q_parts, sink_parts = [], []
        for gq in range(SWA_GROUP):
            hq = hk * SWA_GROUP + gq
            tile = q_ref[0, :, (hq // 2) * LANES:(hq // 2 + 1) * LANES]
            q_parts.append(jnp.where(low_head, tile, zero) if hq % 2 == 0 else jnp.where(low_head, zero, tile))
            sink_parts.append(jnp.broadcast_to(sink_ref[:, hq:hq + 1] * LOG2_E, (w, LANES)))
        q3 = jnp.concatenate(q_parts, axis=0)
        sink = jnp.concatenate(sink_parts, axis=0)
        s = lax.dot_general(q3, kk, contract, preferred_element_type=F32)
        s = jnp.where(valid, s, NEG_INF)
        m = jnp.maximum(jnp.max(s, axis=-1, keepdims=True), sink)
        e = jnp.exp2(s - jnp.concatenate([m, m], axis=1))
        den = jnp.sum(e, axis=-1, keepdims=True) + jnp.exp2(sink - m)
        o = jnp.dot(e.astype(BF16), vv, preferred_element_type=F32) * (1.0 / den)
        for gq in range(SWA_GROUP):
            heads[hk * SWA_GROUP + gq] = o[gq * w:(gq + 1) * w]
    tiles = []
    ss = jnp.zeros((w, 1), F32)
    for t in range(SWA_Q_HEADS // 2):
        tile = jnp.where(low_head, heads[2 * t], heads[2 * t + 1])
        ss = ss + jnp.sum(tile * tile, axis=-1, keepdims=True)
        tiles.append(tile)
    scale = lax.rsqrt(ss * (1.0 / SWA_WIDTH) + RMS_EPS)
    o_ref[0] = (jnp.concatenate(tiles, axis=1) * scale * g_ref[...]).astype(BF16)


def _swa(q3, k3, v3, sinks, out_g):
    b, s, _ = q3.shape
    w = SWA_WINDOW
    kvw = k3.shape[2]
    cur = lambda width: pl.BlockSpec((1, w, width), lambda bi, n: (bi, n, 0))
    prev = lambda width: pl.BlockSpec((1, w, width), lambda bi, n: (bi, jnp.maximum(n - 1, 0), 0))
    return pl.pallas_call(
        _swa_kernel,
        grid=(b, s // w),
        in_specs=[cur(SWA_WIDTH), cur(kvw), prev(kvw), cur(kvw), prev(kvw),
                  _resident((1, SWA_Q_HEADS)), _resident((1, SWA_WIDTH))],
        out_specs=cur(SWA_WIDTH),
        out_shape=jax.ShapeDtypeStruct((b, s, SWA_WIDTH), BF16),
        compiler_params=_params("parallel", "arbitrary"),
        name="swa",
    )(q3, k3, k3, v3, v3, sinks, out_g)


def _diff_kernel(q_ref, k_ref, v_ref, bias_ref, lq1_ref, lk1_ref, lq2_ref, lk2_ref, g_ref, o_ref,
                 q2_ref, m_ref, l_ref, acc_ref, *, lambda_init):
    tq = q_ref.shape[1]
    tk = tq
    i = pl.program_id(2)
    contract = (((1,), (1,)), ((), ()))
    q = q_ref[0]
    low_head = lax.broadcasted_iota(I32, q.shape, 1) < HEAD_DIM
    zero = jnp.zeros(q.shape, q.dtype)
    q2_ref[0:tq, :] = jnp.where(low_head, q, zero)
    q2_ref[tq:2 * tq, :] = jnp.where(low_head, zero, q)
    m_ref[...] = jnp.full(m_ref.shape, NEG_INF, F32)
    l_ref[...] = jnp.zeros(l_ref.shape, F32)
    acc_ref[...] = jnp.zeros(acc_ref.shape, F32)
    reps = tk // LANES

    def step(j, masked):
        k0 = pl.multiple_of(j * tk, tk)
        kblk = k_ref[0, pl.ds(k0, tk), :]
        vblk = v_ref[0, pl.ds(k0, tk), :]
        s = lax.dot_general(q2_ref[...], kblk, contract, preferred_element_type=F32)
        if masked:
            s = s + bias_ref[...]
        m_prev = m_ref[...]
        m_next = jnp.maximum(m_prev, jnp.max(s, axis=-1, keepdims=True))
        alpha = jnp.exp2(m_prev - m_next)
        p = jnp.exp2(s - jnp.concatenate([m_next] * reps, axis=1))
        l_ref[...] = alpha * l_ref[...] + jnp.sum(p, axis=-1, keepdims=True)
        acc_ref[...] = alpha * acc_ref[...] + jnp.dot(p.astype(BF16), vblk, preferred_element_type=F32)
        m_ref[...] = m_next

    def body(j, carry):
        step(j, False)
        return carry

    lax.fori_loop(0, i, body, 0)
    step(i, True)

    lam = (jnp.exp(jnp.sum(lq1_ref[...] * lk1_ref[...], axis=-1, keepdims=True))
           - jnp.exp(jnp.sum(lq2_ref[...] * lk2_ref[...], axis=-1, keepdims=True)) + lambda_init)
    o = acc_ref[0:tq, :] / l_ref[0:tq, :] - lam * (acc_ref[tq:2 * tq, :] / l_ref[tq:2 * tq, :])
    ms = jnp.mean(o * o, axis=-1, keepdims=True)
    o_ref[0] = (o * lax.rsqrt(ms + LN_EPS) * g_ref[...] * (1.0 - lambda_init)).astype(BF16)


def _diff(q3, k3, v3, lq1, lk1, lq2, lk2, subln_g, lambda_init):
    b, s, _ = q3.shape
    tq = min(DIFF_ROWS, s)
    hw = 2 * HEAD_DIM
    qspec = pl.BlockSpec((1, tq, hw), lambda bi, h, i: (bi, i, h))
    kvspec = pl.BlockSpec((1, s, hw), lambda bi, h, i: (bi, 0, h))
    vec = _resident((1, HEAD_DIM))
    qrow = lax.broadcasted_iota(I32, (2, tq, tq), 1).reshape(2 * tq, tq)
    kcol = lax.broadcasted_iota(I32, (2 * tq, tq), 1)
    bias = jnp.where(kcol <= qrow, 0.0, NEG_INF).astype(F32)
    return pl.pallas_call(
        functools.partial(_diff_kernel, lambda_init=lambda_init),
        grid=(b, DIFF_HEADS, s // tq),
        in_specs=[qspec, kvspec, kvspec, _resident((2 * tq, tq)), vec, vec, vec, vec, _resident((1, hw))],
        out_specs=qspec,
        out_shape=jax.ShapeDtypeStruct((b, s, DIFF_WIDTH), BF16),
        scratch_shapes=[pltpu.VMEM((2 * tq, hw), BF16), pltpu.VMEM((2 * tq, hw), F32),
                        pltpu.VMEM((2 * tq, hw), F32), pltpu.VMEM((2 * tq, hw), F32)],
        compiler_params=_params("parallel", "parallel", "arbitrary"),
        name="diff",
    )(q3, k3, v3, bias, lq1, lk1, lq2, lk2, subln_g)


def _bits(v):
    return lax.bitcast_convert_type(v, U32)


def _route(logits):
    lane_i = lax.broadcasted_iota(I32, logits.shape, 1)
    lane = lane_i.astype(F32)

    def first_lane(mask):
        return -jnp.max(jnp.where(mask, -lane, -float(ROUTE_LANES)), axis=-1, keepdims=True)

    is_group = lane_i < N_GROUPS
    gmax = jnp.max(jnp.where(is_group, logits, NEG_INF), axis=-1, keepdims=True)
    g_idx = first_lane(is_group & (logits == gmax))
    g_w = 1.0 / jnp.sum(jnp.where(is_group, jnp.exp(logits - gmax), 0.0), axis=-1, keepdims=True)
    lo = N_GROUPS + g_idx * EXPERTS_PER_GROUP
    in_sel = (lane >= lo) & (lane < lo + EXPERTS_PER_GROUP)
    e1 = jnp.max(jnp.where(in_sel, logits, NEG_INF), axis=-1, keepdims=True)
    i1 = first_lane(in_sel & (logits == e1))
    rest = in_sel & (lane != i1)
    e2 = jnp.max(jnp.where(rest, logits, NEG_INF), axis=-1, keepdims=True)
    i2 = first_lane(rest & (logits == e2))
    t = jnp.exp(e2 - e1)
    w1 = 1.0 / (1.0 + t)
    w2 = t * w1
    out = jnp.where(lane_i == 0, i1 - N_GROUPS,
          jnp.where(lane_i == 1, i2 - N_GROUPS,
          jnp.where(lane_i == 2, g_w * w1,
          jnp.where(lane_i == 3, g_w * w2, 0.0))))
    return out


def _pack_pair(lo_f32, hi_f32):
    lo = _bits(lo_f32.astype(BF16).astype(F32)) >> 16
    hi = _bits(hi_f32.astype(BF16).astype(F32)) & jnp.uint32(0xFFFF0000)
    return lo | hi


def _unpack_pair(words):
    lo = lax.bitcast_convert_type(words << 16, F32)
    hi = lax.bitcast_convert_type(words & jnp.uint32(0xFFFF0000), F32)
    return lo, hi


def _outproj_kernel(x_ref, c_ref, s_ref, d_ref, w_ref, g_ref, wr_ref, br_ref,
                    xo_ref, hp_ref, route_ref, counts_ref, run_ref):
    tm = x_ref.shape[0]

    @pl.when(pl.program_id(0) == 0)
    def _():
        run_ref[...] = jnp.zeros(run_ref.shape, F32)

    nc = 4 * LANES
    n_chunks = D_MODEL // nc
    half = D_MODEL // 2
    sub = tm // OUTPROJ_SPLIT
    tri = (lax.broadcasted_iota(I32, (sub, sub), 1) < lax.broadcasted_iota(I32, (sub, sub), 0)).astype(BF16)

    def main_chunk(rs, mix, j, ss, acc):
        cs = slice(j * nc, (j + 1) * nc)
        xn = x_ref[rs, cs] + jnp.dot(mix, w_ref[:, cs], preferred_element_type=F32)
        xo_ref[rs, cs] = xn
        ss = ss + jnp.sum(xn * xn, axis=-1, keepdims=True)
        head = xn.astype(BF16)
        tail = (xn - head.astype(F32)).astype(BF16)
        wrj = wr_ref[j * nc:(j + 1) * nc, :]
        acc = (acc + jnp.dot(head, wrj, preferred_element_type=F32)
               + jnp.dot(tail, wrj, preferred_element_type=F32))
        return ss, acc

    def make_epilogue(r0, ss, acc):
        rs = slice(r0, r0 + sub)
        scale = lax.rsqrt(ss * (1.0 / D_MODEL) + RMS_EPS)
        wpc = half // n_chunks

        def piece(j):
            lo = slice(j * wpc, (j + 1) * wpc)
            hi = slice(half + j * wpc, half + (j + 1) * wpc)
            words = _pack_pair(xo_ref[rs, lo] * scale * g_ref[:, lo], xo_ref[rs, hi] * scale * g_ref[:, hi])
            for kk in range(wpc // LANES):
                k = j * (wpc // LANES) + kk
                hp_ref[pl.ds(r0 * TOKEN_TILE_ROWS + k, sub, stride=TOKEN_TILE_ROWS), :] = (
                    words[:, kk * LANES:(kk + 1) * LANES])
            if j < n_chunks - 1:
                return
            logits = (acc[:, 0:ROUTE_LANES] + acc[:, ROUTE_LANES:2 * ROUTE_LANES]) * scale + br_ref[...]
            route = _route(logits)
            lane = lax.broadcasted_iota(I32, route.shape, 1)
            lane_f = lane.astype(F32)
            hot1 = (lane_f == route[:, 0:1]).astype(F32)
            hot2 = (lane_f == route[:, 1:2]).astype(F32)
            hot = hot1 + hot2
            before = jnp.dot(tri, hot.astype(BF16), preferred_element_type=F32) + run_ref[...]
            rank1 = jnp.sum(hot1 * before, axis=-1, keepdims=True)
            rank2 = jnp.sum(hot2 * before, axis=-1, keepdims=True)
            route_ref[rs, :] = jnp.where(lane == 4, rank1, jnp.where(lane == 5, rank2, route))
            run_ref[...] = run_ref[...] + jnp.sum(hot, axis=0, keepdims=True)
        return piece

    pending = None
    for r0 in range(0, tm, sub):
        rs = slice(r0, r0 + sub)
        mix = jnp.concatenate([c_ref[rs, :], s_ref[rs, :], d_ref[rs, :]], axis=1)
        ss = jnp.zeros((sub, 1), F32)
        acc = jnp.zeros((sub, 2 * ROUTE_LANES), F32)
        for j in range(n_chunks):
            ss, acc = main_chunk(rs, mix, j, ss, acc)
            if pending is not None:
                pending(j)
        pending = make_epilogue(r0, ss, acc)
    for j in range(n_chunks):
        pending(j)
    counts_ref[...] = run_ref[...]


def _outproj(x2, conv_o, swa_o, diff_o, w_stack, layer, ffn_g, wr_bf, br):
    t = x2.shape[0]
    tm = min(OUTPROJ_ROWS, t)
    row = lambda w: pl.BlockSpec((tm, w), lambda i: (i, 0))
    return pl.pallas_call(
        _outproj_kernel,
        grid=(t // tm,),
        in_specs=[row(D_MODEL), row(CONV_CH), row(SWA_WIDTH), row(DIFF_WIDTH),
                  _resident_layer((D_MODEL, D_MODEL), layer), _resident((1, D_MODEL)),
                  _resident((D_MODEL, 2 * ROUTE_LANES)), _resident((1, ROUTE_LANES))],
        out_specs=[row(D_MODEL), pl.BlockSpec((tm * TOKEN_TILE_ROWS, LANES), lambda i: (i, 0)),
                   row(ROUTE_LANES), pl.BlockSpec((1, ROUTE_LANES), lambda i: (0, 0))],
        out_shape=[jax.ShapeDtypeStruct((t, D_MODEL), F32),
                   jax.ShapeDtypeStruct((t * TOKEN_TILE_ROWS, LANES), U32),
                   jax.ShapeDtypeStruct((t, ROUTE_LANES), F32),
                   jax.ShapeDtypeStruct((1, ROUTE_LANES), F32)],
        scratch_shapes=[pltpu.VMEM((1, ROUTE_LANES), F32)],
        compiler_params=_params("arbitrary"),
        name="outproj",
    )(x2, conv_o, swa_o, diff_o, w_stack, ffn_g, wr_bf, br)


def _dispatch_kernel(slot_ref, hp_ref, init_ref, xs_ref, sem):
    del init_ref
    n = slot_ref.shape[2]
    rows = TOKEN_TILE_ROWS

    def copy(src_row, dst_row):
        return pltpu.make_async_copy(hp_ref.at[pl.ds(src_row, rows), :], xs_ref.at[pl.ds(dst_row, rows), :], sem)

    def start(t, c):
        src = pl.multiple_of(t * rows, rows)
        for k in range(2):
            dst = pl.multiple_of(slot_ref[0, 0, 2 * t + k] * rows, rows)
            copy(src, dst).start(priority=k)
        return c
    lax.fori_loop(0, n // 2, start, 0, unroll=DMA_UNROLL)

    def wait(a, c):
        copy(0, 0).wait()
        return c
    lax.fori_loop(0, n, wait, 0, unroll=2 * DMA_UNROLL)


def _dispatch(slot3, hp, xs_init):
    nt, _, n = slot3.shape
    td = n // 2
    return pl.pallas_call(
        _dispatch_kernel,
        grid=(nt,),
        in_specs=[pl.BlockSpec((1, 1, n), lambda i: (i, 0, 0), memory_space=pltpu.SMEM),
                  pl.BlockSpec((td * TOKEN_TILE_ROWS, LANES), lambda i: (i, 0)),
                  pl.BlockSpec(memory_space=pl.ANY)],
        out_specs=pl.BlockSpec(memory_space=pl.ANY),
        out_shape=jax.ShapeDtypeStruct(xs_init.shape, xs_init.dtype),
        scratch_shapes=[pltpu.SemaphoreType.DMA],
        input_output_aliases={2: 0},
        compiler_params=_params("arbitrary"),
        name="dispatch",
    )(slot3, hp, xs_init)


def _experts_kernel(be_ref, bfirst_ref, bslot_ref, bnext_ref, bnext2_ref, bcnt_ref, nvalid_ref,
                    xs_ref, w13_hbm, w2_hbm, ys_ref,
                    w13_buf, w2_buf, w13_bf, w2_bf, sem13, sem2, *, layer):
    del nvalid_ref
    i = pl.program_id(0)
    e = be_ref[i]
    valid = e >= 0
    slot = bslot_ref[i]

    def w_copies(expert, s):
        copies = []
        r13 = D_MODEL // W13_CHUNKS
        for c in range(W13_CHUNKS):
            rs = pl.ds(c * r13, r13)
            copies.append(pltpu.make_async_copy(w13_hbm.at[layer, expert, rs, :], w13_buf.at[s, rs, :],
                                                sem13.at[s]))
        r2 = EXPERT_FF // W2_CHUNKS
        for c in range(W2_CHUNKS):
            rs = pl.ds(c * r2, r2)
            copies.append(pltpu.make_async_copy(w2_hbm.at[layer, expert, rs, :], w2_buf.at[s, rs, :],
                                                sem2.at[s]))
        return copies

    def start_all(copies):
        for c, cp in enumerate(copies):
            cp.start(priority=c % 2)

    @pl.when(valid & (bfirst_ref[i] == 1))
    def _():
        nxt = bnext_ref[i]
        nxt2 = bnext2_ref[i]

        @pl.when(i == 0)
        def _():
            start_all(w_copies(e, slot))

            @pl.when(nxt >= 0)
            def _():
                start_all(w_copies(nxt, 1 - slot))
        for cp in w_copies(e, slot):
            cp.wait()
        def cast13(c, carry):
            rs = pl.ds(pl.multiple_of(c * CAST_ROWS, CAST_ROWS), CAST_ROWS)
            w13_bf[rs, :] = w13_buf[slot, rs, :].astype(BF16)
            return carry
        lax.fori_loop(0, D_MODEL // CAST_ROWS, cast13, 0)

        def cast2(c, carry):
            rs = pl.ds(pl.multiple_of(c * CAST_ROWS, CAST_ROWS), CAST_ROWS)
            w2_bf[rs, :] = w2_buf[slot, rs, :].astype(BF16)
            return carry
        lax.fori_loop(0, EXPERT_FF // CAST_ROWS, cast2, 0)

        @pl.when(nxt2 >= 0)
        def _():
            start_all(w_copies(nxt2, slot))

    @pl.when(valid)
    def _():
        half = D_MODEL // 2
        tm = xs_ref.shape[0] // TOKEN_TILE_ROWS
        live = lax.broadcasted_iota(I32, (tm, LANES), 0) < bcnt_ref[i]
        los, his = [], []
        for k in range(TOKEN_TILE_ROWS):
            words = jnp.where(live, xs_ref[pl.ds(k, tm, stride=TOKEN_TILE_ROWS), :], jnp.uint32(0))
            lo, hi = _unpack_pair(words)
            los.append(lo.astype(BF16))
            his.append(hi.astype(BF16))
        x_lo = jnp.concatenate(los, axis=1)
        x_hi = jnp.concatenate(his, axis=1)
        h = (jnp.dot(x_lo, w13_bf[0:half, :], preferred_element_type=F32)
             + jnp.dot(x_hi, w13_bf[half:D_MODEL, :], preferred_element_type=F32))
        a, b = h[:, 0:EXPERT_FF], h[:, EXPERT_FF:2 * EXPERT_FF]
        act = (a * jax.nn.sigmoid(a) * b).astype(BF16)
        y = jnp.dot(act, w2_bf[...], preferred_element_type=F32)
        for k in range(TOKEN_TILE_ROWS):
            ys_ref[pl.ds(k, tm, stride=TOKEN_TILE_ROWS), :] = _pack_pair(
                y[:, k * LANES:(k + 1) * LANES], y[:, half + k * LANES:half + (k + 1) * LANES])


def _experts(tables, xs, w13, w2, layer):
    nblk = tables[0].shape[0]
    rows = xs.shape[0] // nblk
    blk = pl.BlockSpec((rows, LANES), lambda i, *tb: (jnp.minimum(i, tb[-1][0] - 1), 0))
    return pl.pallas_call(
        functools.partial(_experts_kernel, layer=layer),
        grid_spec=pltpu.PrefetchScalarGridSpec(
            num_scalar_prefetch=len(tables),
            grid=(nblk,),
            in_specs=[blk, pl.BlockSpec(memory_space=pl.ANY), pl.BlockSpec(memory_space=pl.ANY)],
            out_specs=blk,
            scratch_shapes=[pltpu.VMEM((2, D_MODEL, 2 * EXPERT_FF), F32),
                            pltpu.VMEM((2, EXPERT_FF, D_MODEL), F32),
                            pltpu.VMEM((D_MODEL, 2 * EXPERT_FF), BF16),
                            pltpu.VMEM((EXPERT_FF, D_MODEL), BF16),
                            pltpu.SemaphoreType.DMA((2,)), pltpu.SemaphoreType.DMA((2,))],
        ),
        out_shape=jax.ShapeDtypeStruct(xs.shape, U32),
        input_output_aliases={len(tables): 0},
        compiler_params=_params("arbitrary"),
        name="experts",
    )(*tables, xs, w13, w2)


def _combine_kernel(slot_ref, slot_next_ref, x_ref, route_ref, ys_hbm, g_ref, o_ref, buf, sem, *, final_norm):
    tm = x_ref.shape[0]
    n = slot_ref.shape[2]
    rows = TOKEN_TILE_ROWS
    half = D_MODEL // 2
    i = pl.program_id(0)
    cur = i % 2

    def copy(src_row, b, dst_row):
        return pltpu.make_async_copy(ys_hbm.at[pl.ds(src_row, rows), :], buf.at[b, pl.ds(dst_row, rows), :],
                                     sem.at[b])

    def request(slots, b):
        def start(t, c):
            for k in range(2):
                a = k * tm + t
                src = pl.multiple_of(slots[0, 0, a] * rows, rows)
                copy(src, b, pl.multiple_of(a * rows, rows)).start(priority=k)
            return c
        lax.fori_loop(0, tm, start, 0, unroll=DMA_UNROLL)

    @pl.when(i == 0)
    def _():
        request(slot_ref, 0)

    @pl.when(i + 1 < pl.num_programs(0))
    def _():
        request(slot_next_ref, 1 - cur)

    def wait(a, c):
        copy(0, cur, 0).wait()
        return c
    lax.fori_loop(0, n, wait, 0, unroll=2 * DMA_UNROLL)

    g1 = route_ref[:, 2:3]
    g2 = route_ref[:, 3:4]
    ss = jnp.zeros((tm, 1), F32)
    for k in range(rows):
        lo1, hi1 = _unpack_pair(buf[cur, pl.ds(k, tm, stride=rows), :])
        lo2, hi2 = _unpack_pair(buf[cur, pl.ds(tm * rows + k, tm, stride=rows), :])
        c_lo = slice(k * LANES, (k + 1) * LANES)
        c_hi = slice(half + k * LANES, half + (k + 1) * LANES)
        x_lo = x_ref[:, c_lo] + (g1 * lo1 + g2 * lo2)
        x_hi = x_ref[:, c_hi] + (g1 * hi1 + g2 * hi2)
        o_ref[:, c_lo] = x_lo
        o_ref[:, c_hi] = x_hi
        if final_norm:
            ss = (ss + jnp.sum(x_lo * x_lo, axis=-1, keepdims=True)
                  + jnp.sum(x_hi * x_hi, axis=-1, keepdims=True))
    if final_norm:
        scale = lax.rsqrt(ss * (1.0 / D_MODEL) + RMS_EPS)
        o_ref[...] = o_ref[...] * scale * g_ref[...]


def _combine(slot3, x2, route, ys, final_g, final_norm):
    t = x2.shape[0]
    nt, _, n = slot3.shape
    tm = n // 2
    row = lambda w: pl.BlockSpec((tm, w), lambda i: (i, 0))
    return pl.pallas_call(
        functools.partial(_combine_kernel, final_norm=final_norm),
        grid=(nt,),
        in_specs=[pl.BlockSpec((1, 1, n), lambda i: (i, 0, 0), memory_space=pltpu.SMEM),
                  pl.BlockSpec((1, 1, n), lambda i: (jnp.minimum(i + 1, nt - 1), 0, 0),
                               memory_space=pltpu.SMEM),
                  row(D_MODEL), row(ROUTE_LANES), pl.BlockSpec(memory_space=pl.ANY),
                  _resident((1, D_MODEL))],
        out_specs=row(D_MODEL),
        out_shape=jax.ShapeDtypeStruct((t, D_MODEL), F32),
        scratch_shapes=[pltpu.VMEM((2, n * TOKEN_TILE_ROWS, LANES), U32), pltpu.SemaphoreType.DMA((2,))],
        compiler_params=_params("arbitrary"),
        name="combine",
    )(slot3, slot3, x2, route, ys, final_g)


def _dispatch_tables(route, counts, t):
    tm = MOE_ROWS
    ids = route[:, 0:2].astype(I32)
    rank = route[:, 4:6].astype(I32)
    counts = counts[0, 0:N_EXPERTS].astype(I32)
    padded = (counts + tm - 1) // tm * tm
    pad_end = jnp.cumsum(padded)
    pad_start = pad_end - padded
    experts = jnp.arange(N_EXPERTS, dtype=I32)
    slot = rank + jnp.sum(jnp.where(ids[..., None] == experts, pad_start, 0), axis=-1)
    nblk = 2 * t // tm + N_EXPERTS
    blk_row = jnp.arange(nblk, dtype=I32) * tm
    blk_valid = blk_row < pad_end[-1]
    be = jnp.minimum(jnp.sum((blk_row[:, None] >= pad_end[None, :]).astype(I32), axis=1), N_EXPERTS - 1)
    onehot = (be[:, None] == experts[None, :]).astype(I32)
    pick = lambda table: jnp.sum(onehot * table[None, :], axis=1)
    blk_e = jnp.where(blk_valid, be, -1)
    blk_off = blk_row - pick(pad_start)
    blk_first = (blk_valid & (blk_off == 0)).astype(I32)
    blk_cnt = jnp.where(blk_valid, jnp.clip(pick(counts) - blk_off, 0, tm), 0)
    used = counts > 0
    order = jnp.cumsum(used.astype(I32)) - 1
    cand = jnp.where(used[None, :] & (experts[None, :] > experts[:, None]), experts[None, :], N_EXPERTS)
    nxt = jnp.min(cand, axis=1)
    nxt = jnp.where(nxt >= N_EXPERTS, -1, nxt)
    nxt2 = jnp.where(nxt >= 0, jnp.sum(jnp.where(experts[None, :] == nxt[:, None], nxt[None, :], 0), axis=1), -1)
    nvalid = pad_end[-1:] // tm
    tables = (blk_e, blk_first, pick(order) % 2, pick(nxt), pick(nxt2), blk_cnt, nvalid)
    return slot, tuple(tb.astype(I32) for tb in tables)


def _moe(x2, hp, route, counts, w13, w2, layer, final_g, final_norm, sorted_buf):
    t = x2.shape[0]
    slot, tables = _dispatch_tables(route, counts, t)
    td = min(DISPATCH_ROWS, t)
    nblk = tables[0].shape[0]
    if sorted_buf is None:
        sorted_buf = jnp.zeros((nblk * MOE_ROWS * TOKEN_TILE_ROWS, LANES), U32)
    xs = _dispatch(slot.reshape(t // td, 1, 2 * td), hp, sorted_buf)
    ys = _experts(tables, xs, w13, w2, layer)
    tc = min(COMBINE_ROWS, t)
    slot3 = slot.reshape(t // tc, tc, 2).transpose(0, 2, 1).reshape(t // tc, 1, 2 * tc)
    return _combine(slot3, x2, route, ys, final_g, final_norm), ys


def _rope_tables(positions):
    inv_freq = ROPE_THETA ** (-jnp.arange(0, HEAD_DIM, 2, dtype=F32) / HEAD_DIM)
    reps = LANES // inv_freq.shape[0]
    ang = positions.astype(F32).reshape(-1, 1) * jnp.tile(inv_freq, reps)[None, :]
    sign = jnp.tile(jnp.concatenate([-jnp.ones_like(inv_freq), jnp.ones_like(inv_freq)]), reps // 2)
    return jnp.cos(ang), jnp.sin(ang) * sign[None, :]


def _router_weights(wg, we, bg, be, norm_g):
    w = jnp.zeros((D_MODEL, ROUTE_LANES), F32).at[:, 0:N_GROUPS].set(wg)
    w = w.at[:, N_GROUPS:N_GROUPS + N_EXPERTS].set(we)
    w = w * norm_g[:, None]
    head = w.astype(BF16)
    tail = (w - head.astype(F32)).astype(BF16)
    bias = jnp.zeros((1, ROUTE_LANES), F32).at[0, 0:N_GROUPS].set(bg)
    bias = bias.at[0, N_GROUPS:N_GROUPS + N_EXPERTS].set(be)
    return jnp.concatenate([head, tail], axis=1), bias


def kernel(x, positions, attn_norm_g, w_in, conv_dw_w, conv_dw_b, conv_ln_g, conv_ln_b, conv_pw_w, conv_out_g, swa_sinks, swa_out_g, diff_lambda_q1, diff_lambda_k1, diff_lambda_q2, diff_lambda_k2, diff_subln_g, w_out, ffn_norm_g, router_group_w, router_group_b, router_expert_w, router_expert_b, moe_w13, moe_w2, final_norm_g):
    b, s, d = x.shape
    t = b * s
    depth = w_in.shape[0]
    cos128, sin128 = _rope_tables(positions)
    x2 = x.reshape(t, d)
    row = lambda v: v.reshape(1, -1)
    w_in_bf = w_in.astype(BF16)
    w_out_bf = w_out.astype(BF16)
    sorted_buf = None
    for i in range(depth):
        lambda_init = 0.8 - 0.6 * float(np.exp(-0.3 * i))
        z, qs, ks, vs, qd, kd, vd = _inproj(x2, row(attn_norm_g[i]), w_in_bf, i, cos128, sin128)
        conv_o = _conv(z.reshape(b, s, CONV_CH), conv_dw_w[i], row(conv_dw_b[i]), row(conv_ln_g[i]),
                       row(conv_ln_b[i]), conv_pw_w[i].astype(BF16), row(conv_out_g[i]))
        swa_o = _swa(qs.reshape(b, s, -1), ks.reshape(b, s, -1), vs.reshape(b, s, -1),
                     row(swa_sinks[i]), row(swa_out_g[i]))
        diff_o = _diff(qd.reshape(b, s, -1), kd.reshape(b, s, -1), vd.reshape(b, s, -1),
                       row(diff_lambda_q1[i]), row(diff_lambda_k1[i]), row(diff_lambda_q2[i]),
                       row(diff_lambda_k2[i]), row(diff_subln_g[i]), lambda_init)
        wr, br = _router_weights(router_group_w[i], router_expert_w[i], router_group_b[i], router_expert_b[i],
                                 ffn_norm_g[i])
        x2, hp, route, counts = _outproj(x2, conv_o.reshape(t, -1), swa_o.reshape(t, -1),
                                         diff_o.reshape(t, -1), w_out_bf, i, row(ffn_norm_g[i]), wr, br)
        x2, sorted_buf = _moe(x2, hp, route, counts, moe_w13, moe_w2, i, row(final_norm_g), i == depth - 1,
                              sorted_buf)
    return x2.reshape(b, s, d)
```
